```python
import math
import jax, jax.numpy as jnp
from jax import lax
import numpy as np

D_MODEL = 1024
BATCH = 8
SEQ = 4096
DEPTH = 4

GRID_W = 64
Q_BLOCK = 128
N_MIXERS = 3
A_HEADS = 16
A_KV_HEADS = 4
A_HEAD_DIM = D_MODEL // A_HEADS
AXIAL_THETA = 10000.0
FNET_GROUPS = 4
C_HEADS = 8
C_HEAD_DIM = D_MODEL // C_HEADS // 2
PARTIAL_ROPE_DIM = C_HEAD_DIM // 4
ROPE_THETA = 500000.0
D_FF = 2816
CONV_WIDTH = 3
DEEPNORM_ALPHA = (2.0 * DEPTH) ** 0.25
DEEPNORM_BETA = (8.0 * DEPTH) ** -0.25
LN_EPS = 1e-5
RMS_EPS = 1e-6

kernel_name = 'hybrid_interleaved_gqa_fnet_diffattn_encoder'


def _layernorm(x, g, b):
    x32 = x.astype(jnp.float32)
    mu = jnp.mean(x32, axis=-1, keepdims=True)
    var = jnp.mean(jnp.square(x32 - mu), axis=-1, keepdims=True)
    y = (x32 - mu) * lax.rsqrt(var + LN_EPS) * g.astype(jnp.float32) + b.astype(jnp.float32)
    return y.astype(x.dtype)


def _rmsnorm(x, g):
    x32 = x.astype(jnp.float32)
    y = x32 * lax.rsqrt(jnp.mean(jnp.square(x32), axis=-1, keepdims=True) + RMS_EPS) * g.astype(jnp.float32)
    return y.astype(x.dtype)


def _rope_cos_sin(pos, dim, theta):
    inv_freq = theta ** (-jnp.arange(0, dim, 2, dtype=jnp.float32) / dim)
    ang = pos.astype(jnp.float32)[:, None] * inv_freq[None, :]
    return jnp.cos(ang), jnp.sin(ang)


def _rotate_half(x, cos, sin):
    half = x.shape[-1] // 2
    shape = (cos.shape[0],) + (1,) * (x.ndim - 3) + (cos.shape[-1],)
    c = cos.reshape(shape).astype(x.dtype)
    s = sin.reshape(shape).astype(x.dtype)
    x1, x2 = x[..., :half], x[..., half:]
    return jnp.concatenate([x1 * c - x2 * s, x2 * c + x1 * s], axis=-1)


def _axial_rope(x, cos_r, sin_r, cos_c, sin_c):
    half = x.shape[-1] // 2
    return jnp.concatenate([_rotate_half(x[..., :half], cos_r, sin_r),
                            _rotate_half(x[..., half:], cos_c, sin_c)], axis=-1)


def _partial_rope(x, cos, sin):
    return jnp.concatenate([_rotate_half(x[..., :PARTIAL_ROPE_DIM], cos, sin),
                            x[..., PARTIAL_ROPE_DIM:]], axis=-1)


def _to_blocks(q):
    b, s = q.shape[:2]
    return jnp.moveaxis(q.reshape((b, s // Q_BLOCK, Q_BLOCK) + q.shape[2:]), 1, 0)


def _from_blocks(o):
    o = jnp.moveaxis(o, 0, 1)
    return o.reshape((o.shape[0], o.shape[1] * o.shape[2]) + o.shape[3:])


def _lambda_init(layer_idx):
    return 0.8 - 0.6 * math.exp(-0.3 * layer_idx)


def _mixer_gqa_axial(x, w_qkv, q_norm, k_norm, w_o, axial):
    b, s, _ = x.shape
    nq = A_HEADS * A_HEAD_DIM
    nkv = A_KV_HEADS * A_HEAD_DIM
    qkv = x @ w_qkv
    q = qkv[..., :nq].reshape(b, s, A_KV_HEADS, A_HEADS // A_KV_HEADS, A_HEAD_DIM)
    k = qkv[..., nq:nq + nkv].reshape(b, s, A_KV_HEADS, A_HEAD_DIM)
    v = qkv[..., nq + nkv:].reshape(b, s, A_KV_HEADS, A_HEAD_DIM)
    q = _axial_rope(_rmsnorm(q, q_norm), *axial)
    k = _axial_rope(_rmsnorm(k, k_norm), *axial)
    scale = A_HEAD_DIM ** -0.5

    def attend(qb):
        sc = jnp.einsum('bqkgd,bskd->bkgqs', qb, k).astype(jnp.float32) * scale
        p = jax.nn.softmax(sc, axis=-1).astype(v.dtype)
        return jnp.einsum('bkgqs,bskd->bqkgd', p, v)

    o = _from_blocks(lax.map(attend, _to_blocks(q)))
    return o.reshape(b, s, nq) @ w_o


def _mixer_fourier(x, w_o, b_o):
    b, s, d = x.shape
    u = x.astype(jnp.float32).reshape(b, s, FNET_GROUPS, d // FNET_GROUPS)
    f = jnp.fft.fft2(u, axes=(1, 3), norm='ortho').real
    return f.reshape(b, s, d).astype(x.dtype) @ w_o + b_o


def _mixer_diff(x, w_qkv, lq1, lk1, lq2, lk2, subln, w_o, rope, lambda_init):
    b, s, d = x.shape
    q, k, v = jnp.split(x @ w_qkv, 3, axis=-1)
    q = _partial_rope(q.reshape(b, s, C_HEADS, 2, C_HEAD_DIM), *rope)
    k = _partial_rope(k.reshape(b, s, C_HEADS, 2, C_HEAD_DIM), *rope)
    v = v.reshape(b, s, C_HEADS, 2 * C_HEAD_DIM)
    f32 = jnp.float32
    lam = (jnp.exp(jnp.sum(lq1.astype(f32) * lk1.astype(f32)))
           - jnp.exp(jnp.sum(lq2.astype(f32) * lk2.astype(f32))) + lambda_init)
    scale = C_HEAD_DIM ** -0.5

    def attend(qb):
        sc = jnp.einsum('bqhcd,bshcd->bhcqs', qb, k).astype(f32) * scale
        p = jax.nn.softmax(sc, axis=-1)
        a = (p[:, :, 0] - lam * p[:, :, 1]).astype(v.dtype)
        return jnp.einsum('bhqs,bshe->bqhe', a, v)

    o = _from_blocks(lax.map(attend, _to_blocks(q)))
    o = _rmsnorm(o, subln) * (1.0 - lambda_init)
    return o.reshape(b, s, d) @ w_o


def _conv_ffn(x, w_up, conv_w, conv_b, w_down):
    s = x.shape[1]
    h = x @ w_up
    pad = CONV_WIDTH // 2
    hp = jnp.pad(h, ((0, 0), (pad, pad), (0, 0)))
    h = conv_b + sum(hp[:, j:j + s] * conv_w[j] for j in range(CONV_WIDTH))
    gate, val = jnp.split(h, 2, axis=-1)
    return (jax.nn.silu(gate) * val) @ w_down


def _post_norm(x, sub, g, b):
    return _layernorm(DEEPNORM_ALPHA * x + sub, g, b)


def _dense(key, fan_in, fan_out, scale=1.0):
    return jax.random.normal(key, (fan_in, fan_out), jnp.float32) * (scale * fan_in ** -0.5)


def _gain(key, n):
    return 1.0 + 0.01 * jax.random.normal(key, (n,), jnp.float32)


def _bias(key, n):
    return 0.01 * jax.random.normal(key, (n,), jnp.float32)


def setup_inputs(seed: int = 0) -> dict:
    key = jax.random.key(seed)
    key, kx = jax.random.split(key)
    inputs = {'x': jax.random.normal(kx, (BATCH, SEQ, D_MODEL), jnp.float32)}
    qkv_a = (A_HEADS + 2 * A_KV_HEADS) * A_HEAD_DIM
    for i in range(DEPTH):
        ks = jax.random.split(jax.random.fold_in(key, i), 20)
        p = 'l%d_' % i
        kind = i % N_MIXERS
        if kind == 0:
            inputs[p + 'a_wqkv'] = _dense(ks[0], D_MODEL, qkv_a)
            inputs[p + 'a_qnorm'] = _gain(ks[1], A_HEAD_DIM)
            inputs[p + 'a_knorm'] = _gain(ks[2], A_HEAD_DIM)
            inputs[p + 'a_wo'] = _dense(ks[3], A_HEADS * A_HEAD_DIM, D_MODEL, DEEPNORM_BETA)
        elif kind == 1:
            inputs[p + 'f_wo'] = _dense(ks[0], D_MODEL, D_MODEL, DEEPNORM_BETA)
            inputs[p + 'f_bo'] = _bias(ks[1], D_MODEL)
        else:
            inputs[p + 'c_wqkv'] = _dense(ks[0], D_MODEL, 3 * D_MODEL)
            inputs[p + 'c_lq1'] = 0.1 * jax.random.normal(ks[1], (C_HEAD_DIM,), jnp.float32)
            inputs[p + 'c_lk1'] = 0.1 * jax.random.normal(ks[2], (C_HEAD_DIM,), jnp.float32)
            inputs[p + 'c_lq2'] = 0.1 * jax.random.normal(ks[3], (C_HEAD_DIM,), jnp.float32)
            inputs[p + 'c_lk2'] = 0.1 * jax.random.normal(ks[4], (C_HEAD_DIM,), jnp.float32)
            inputs[p + 'c_subln'] = _gain(ks[5], 2 * C_HEAD_DIM)
            inputs[p + 'c_wo'] = _dense(ks[6], D_MODEL, D_MODEL, DEEPNORM_BETA)
        inputs[p + 'ln1_g'] = _gain(ks[10], D_MODEL)
        inputs[p + 'ln1_b'] = _bias(ks[11], D_MODEL)
        inputs[p + 'ffn_wup'] = _dense(ks[12], D_MODEL, 2 * D_FF)
        inputs[p + 'ffn_conv_w'] = jax.random.normal(ks[13], (CONV_WIDTH, 2 * D_FF), jnp.float32) * CONV_WIDTH ** -0.5
        inputs[p + 'ffn_conv_b'] = _bias(ks[14], 2 * D_FF)
        inputs[p + 'ffn_wdown'] = _dense(ks[15], D_FF, D_MODEL, DEEPNORM_BETA)
        inputs[p + 'ln2_g'] = _gain(ks[16], D_MODEL)
        inputs[p + 'ln2_b'] = _bias(ks[17], D_MODEL)
    return inputs


def reference(x,
              l0_a_wqkv, l0_a_qnorm, l0_a_knorm, l0_a_wo,
              l0_ln1_g, l0_ln1_b, l0_ffn_wup, l0_ffn_conv_w, l0_ffn_conv_b, l0_ffn_wdown, l0_ln2_g, l0_ln2_b,
              l1_f_wo, l1_f_bo,
              l1_ln1_g, l1_ln1_b, l1_ffn_wup, l1_ffn_conv_w, l1_ffn_conv_b, l1_ffn_wdown, l1_ln2_g, l1_ln2_b,
              l2_c_wqkv, l2_c_lq1, l2_c_lk1, l2_c_lq2, l2_c_lk2, l2_c_subln, l2_c_wo,
              l2_ln1_g, l2_ln1_b, l2_ffn_wup, l2_ffn_conv_w, l2_ffn_conv_b, l2_ffn_wdown, l2_ln2_g, l2_ln2_b,
              l3_a_wqkv, l3_a_qnorm, l3_a_knorm, l3_a_wo,
              l3_ln1_g, l3_ln1_b, l3_ffn_wup, l3_ffn_conv_w, l3_ffn_conv_b, l3_ffn_wdown, l3_ln2_g, l3_ln2_b):
    s = x.shape[1]
    rows = s // GRID_W
    t_row = jnp.repeat(jnp.arange(rows, dtype=jnp.int32), GRID_W)
    t_col = jnp.tile(jnp.arange(GRID_W, dtype=jnp.int32), rows)
    axial = (_rope_cos_sin(t_row, A_HEAD_DIM // 2, AXIAL_THETA)
             + _rope_cos_sin(t_col, A_HEAD_DIM // 2, AXIAL_THETA))
    rope_c = _rope_cos_sin(jnp.arange(s, dtype=jnp.int32), PARTIAL_ROPE_DIM, ROPE_THETA)

    mixers = ((l0_a_wqkv, l0_a_qnorm, l0_a_knorm, l0_a_wo),
              (l1_f_wo, l1_f_bo),
              (l2_c_wqkv, l2_c_lq1, l2_c_lk1, l2_c_lq2, l2_c_lk2, l2_c_subln, l2_c_wo),
              (l3_a_wqkv, l3_a_qnorm, l3_a_knorm, l3_a_wo))
    norms1 = ((l0_ln1_g, l0_ln1_b), (l1_ln1_g, l1_ln1_b), (l2_ln1_g, l2_ln1_b), (l3_ln1_g, l3_ln1_b))
    ffns = ((l0_ffn_wup, l0_ffn_conv_w, l0_ffn_conv_b, l0_ffn_wdown),
            (l1_ffn_wup, l1_ffn_conv_w, l1_ffn_conv_b, l1_ffn_wdown),
            (l2_ffn_wup, l2_ffn_conv_w, l2_ffn_conv_b, l2_ffn_wdown),
            (l3_ffn_wup, l3_ffn_conv_w, l3_ffn_conv_b, l3_ffn_wdown))
    norms2 = ((l0_ln2_g, l0_ln2_b), (l1_ln2_g, l1_ln2_b), (l2_ln2_g, l2_ln2_b), (l3_ln2_g, l3_ln2_b))

    for i in range(DEPTH):
        kind = i % N_MIXERS
        if kind == 0:
            sub = _mixer_gqa_axial(x, *mixers[i], axial)
        elif kind == 1:
            sub = _mixer_fourier(x, *mixers[i])
        else:
            sub = _mixer_diff(x, *mixers[i], rope_c, _lambda_init(i))
        x = _post_norm(x, sub, *norms1[i])
        x = _post_norm(x, _conv_ffn(x, *ffns[i]), *norms2[i])
    return x
```

```python
import functools
import math

import jax
import jax.numpy as jnp
from jax import lax
from jax.experimental import pallas as pl
from jax.experimental.pallas import tpu as pltpu

F32 = jnp.float32
BF16 = jnp.bfloat16

DEPTH = 4
GRID_W = 64
A_HEADS = 16
A_KV_HEADS = 4
HEAD_DIM = 64
AXIAL_THETA = 10000.0
FNET_GROUPS = 4
C_HEADS = 8
PARTIAL_ROPE_DIM = 16
ROPE_THETA = 500000.0
CONV_WIDTH = 3
DEEPNORM_ALPHA = (2.0 * DEPTH) ** 0.25
LN_EPS = 1e-5
RMS_EPS = 1e-6
ATTN_SCALE = HEAD_DIM ** -0.5

LANES = 128
MXU_DIM = 256
BF16_SUBLANES = 16
VMEM_LIMIT = 56 * 1024 * 1024

ROW_TILE = 512
FFN_CHUNK = 256
HALO = BF16_SUBLANES
KV_CHUNK = 512
ATTN_ROWS = 1024


def _const_spec(shape):
    return pl.BlockSpec(shape, lambda *_: (0,) * len(shape), pipeline_mode=pl.Buffered(1))


def _params(n_axes):
    return pltpu.CompilerParams(dimension_semantics=("arbitrary",) * n_axes,
                                vmem_limit_bytes=VMEM_LIMIT)


def _layernorm(y, g, b):
    mu = jnp.mean(y, axis=-1, keepdims=True)
    d = y - mu
    var = jnp.mean(d * d, axis=-1, keepdims=True)
    return d * lax.rsqrt(var + LN_EPS) * g + b


def _rope(y, c, sa, sb, d):
    return y * c + pltpu.roll(y, LANES - d, 1) * sa + pltpu.roll(y, d, 1) * sb


def _dup_halves(blk):
    swapped = pltpu.roll(blk, HEAD_DIM, 1)
    lo = lax.broadcasted_iota(jnp.int32, blk.shape, 1) < HEAD_DIM
    return jnp.where(lo, blk, swapped), jnp.where(lo, swapped, blk)


def _qkv_a_kernel(x_ref, w_ref, bd_ref, gq_ref, gk_ref, c_ref, sa_ref, sb_ref, q_ref, k_ref, v_ref):
    xb = x_ref[...].astype(BF16)
    c, sa, sb = c_ref[...], sa_ref[...], sb_ref[...]
    bd = bd_ref[...]
    n_q = A_HEADS * HEAD_DIM // MXU_DIM

    def head_norm(y, g):
        sq = y * y
        hi = sq.astype(BF16)
        lo = (sq - hi.astype(F32)).astype(BF16)
        ms = (jnp.dot(hi, bd, preferred_element_type=F32) + jnp.dot(lo, bd, preferred_element_type=F32))
        return y * lax.rsqrt(ms + RMS_EPS) * g

    for j in range(n_q + 2):
        y = jnp.dot(xb, w_ref[:, j * MXU_DIM:(j + 1) * MXU_DIM], preferred_element_type=F32)
        if j < n_q:
            yn = head_norm(y, gq_ref[...])
            for t in range(2):
                r = _rope(yn[:, t * LANES:(t + 1) * LANES], c, sa, sb, HEAD_DIM // 4)
                q_ref[:, j * MXU_DIM + t * LANES:j * MXU_DIM + (t + 1) * LANES] = (r * ATTN_SCALE).astype(BF16)
        elif j == n_q:
            yn = head_norm(y, gk_ref[...])
            for t in range(2):
                r = _rope(yn[:, t * LANES:(t + 1) * LANES], c, sa, sb, HEAD_DIM // 4)
                d0, d1 = _dup_halves(r)
                k_ref[:, (2 * t) * LANES:(2 * t + 1) * LANES] = d0.astype(BF16)
                k_ref[:, (2 * t + 1) * LANES:(2 * t + 2) * LANES] = d1.astype(BF16)
        else:
            for t in range(2):
                d0, d1 = _dup_halves(y[:, t * LANES:(t + 1) * LANES])
                v_ref[:, (2 * t) * LANES:(2 * t + 1) * LANES] = d0.astype(BF16)
                v_ref[:, (2 * t + 1) * LANES:(2 * t + 2) * LANES] = d1.astype(BF16)


def _qkv_a(x2, w, gq, gk, tabs, seq):
    n, d = x2.shape
    tm = ROW_TILE
    n_out = w.shape[1]
    c, sa, sb = tabs
    bd = (jnp.arange(MXU_DIM)[:, None] // HEAD_DIM == jnp.arange(MXU_DIM)[None, :] // HEAD_DIM)
    bd = (bd.astype(F32) / HEAD_DIM).astype(BF16)
    gq2 = jnp.tile(gq, MXU_DIM // HEAD_DIM)[None, :]
    gk2 = jnp.tile(gk, MXU_DIM // HEAD_DIM)[None, :]
    tab_spec = pl.BlockSpec((tm, LANES), lambda i: (i % (seq // tm), 0))
    kvw = A_KV_HEADS * LANES
    return pl.pallas_call(
        _qkv_a_kernel,
        grid=(n // tm,),
        in_specs=[pl.BlockSpec((tm, d), lambda i: (i, 0)),
                  _const_spec((d, n_out)), _const_spec((MXU_DIM, MXU_DIM)),
                  _const_spec((1, MXU_DIM)), _const_spec((1, MXU_DIM)),
                  tab_spec, tab_spec, tab_spec],
        out_specs=[pl.BlockSpec((tm, d), lambda i: (i, 0)),
                   pl.BlockSpec((tm, kvw), lambda i: (i, 0)),
                   pl.BlockSpec((tm, kvw), lambda i: (i, 0))],
        out_shape=[jax.ShapeDtypeStruct((n, d), BF16),
                   jax.ShapeDtypeStruct((n, kvw), BF16),
                   jax.ShapeDtypeStruct((n, kvw), BF16)],
        compiler_params=_params(1),
        name="qkv_gqa",
    )(x2, w.astype(BF16), bd, gq2, gk2, c, sa, sb)


def _flash(qs, k_ref, v_ref, seq):
    m_rows = qs.shape[0]

    def body(ci, carry):
        m, l, acc = carry
        off = pl.multiple_of(ci * KV_CHUNK, KV_CHUNK)
        k = k_ref[0, pl.ds(off, KV_CHUNK), :]
        v = v_ref[0, pl.ds(off, KV_CHUNK), :]
        s = lax.dot_general(qs, k, (((1,), (1,)), ((), ())), preferred_element_type=F32)
        m_new = jnp.maximum(m, jnp.max(s, axis=-1, keepdims=True))
        alpha = jnp.exp(m - m_new)
        p = jnp.exp(s - m_new)
        l = alpha * l + jnp.sum(p, axis=-1, keepdims=True)
        acc = alpha * acc + jnp.dot(p.astype(BF16), v, preferred_element_type=F32)
        return m_new, l, acc

    init = (jnp.full((m_rows, 1), -jnp.inf, F32), jnp.zeros((m_rows, 1), F32),
            jnp.zeros((m_rows, LANES), F32))
    _, l, acc = lax.fori_loop(0, seq // KV_CHUNK, body, init)
    return acc / l


def _attn_a_kernel(q_ref, k_ref, v_ref, o_ref, *, seq, tq):
    lo = lax.broadcasted_iota(jnp.int32, (tq, LANES), 1) < HEAD_DIM
    parts = []
    for h in range(4):
        qp = q_ref[0, :, (h // 2) * LANES:(h // 2 + 1) * LANES].astype(F32)
        keep = lo if h % 2 == 0 else jnp.logical_not(lo)
        parts.append(jnp.where(keep, qp, 0.0).astype(BF16))
    o = _flash(jnp.concatenate(parts, axis=0), k_ref, v_ref, seq)
    o_ref[0, :, 0:LANES] = jnp.where(lo, o[0:tq], o[tq:2 * tq]).astype(BF16)
    o_ref[0, :, LANES:2 * LANES] = jnp.where(lo, o[2 * tq:3 * tq], o[3 * tq:4 * tq]).astype(BF16)


def _attn_a(q, kd, vd):
    b, seq, d = q.shape
    tq = ATTN_ROWS // 4
    return pl.pallas_call(
        functools.partial(_attn_a_kernel, seq=seq, tq=tq),
        grid=(b, A_KV_HEADS, seq // tq),
        in_specs=[pl.BlockSpec((1, tq, MXU_DIM), lambda bi, g, i: (bi, i, g)),
                  pl.BlockSpec((1, seq, LANES), lambda bi, g, i: (bi, 0, g)),
                  pl.BlockSpec((1, seq, LANES), lambda bi, g, i: (bi, 0, g))],
        out_specs=pl.BlockSpec((1, tq, MXU_DIM), lambda bi, g, i: (bi, i, g)),
        out_shape=jax.ShapeDtypeStruct((b, seq, d), BF16),
        compiler_params=_params(3),
        name="attn_gqa",
    )(q, kd, vd)


def _attn_c_kernel(lqk_ref, subln_ref, q_ref, k_ref, v_ref, o_ref, *, seq, tq, lambda_init):
    lqk = lqk_ref[...]
    lam = (jnp.exp(jnp.sum(lqk[0:1] * lqk[1:2], axis=-1, keepdims=True))
           - jnp.exp(jnp.sum(lqk[2:3] * lqk[3:4], axis=-1, keepdims=True)) + lambda_init)
    lo = lax.broadcasted_iota(jnp.int32, (tq, LANES), 1) < HEAD_DIM
    qp = q_ref[0].astype(F32)
    qs = jnp.concatenate([jnp.where(lo, qp, 0.0).astype(BF16),
                          jnp.where(lo, 0.0, qp).astype(BF16)], axis=0)
    o = _flash(qs, k_ref, v_ref, seq)
    o = o[0:tq] - lam * o[tq:2 * tq]
    ms = jnp.mean(o * o, axis=-1, keepdims=True)
    o = o * lax.rsqrt(ms + RMS_EPS) * subln_ref[...] * (1.0 - lambda_init)
    o_ref[0] = o.astype(BF16)


def _attn_c(q, k, v, lqk, subln, lambda_init):
    b, seq, d = q.shape
    tq = ATTN_ROWS // 2
    blk = lambda bi, h, i: (bi, i, h)
    kv = lambda bi, h, i: (bi, 0, h)
    return pl.pallas_call(
        functools.partial(_attn_c_kernel, seq=seq, tq=tq, lambda_init=lambda_init),
        grid=(b, C_HEADS, seq // tq),
        in_specs=[_const_spec((4, HEAD_DIM)), _const_spec((1, LANES)),
                  pl.BlockSpec((1, tq, LANES), blk),
                  pl.BlockSpec((1, seq, LANES), kv),
                  pl.BlockSpec((1, seq, LANES), kv)],
        out_specs=pl.BlockSpec((1, tq, LANES), blk),
        out_shape=jax.ShapeDtypeStruct((b, seq, d), BF16),
        compiler_params=_params(3),
        name="attn_diff",
    )(lqk, subln[None, :], q, k, v)


def _qkv_c_kernel(x_ref, w_ref, c_ref, sa_ref, sb_ref, q_ref, k_ref, v_ref):
    xb = x_ref[...].astype(BF16)
    c, sa, sb = c_ref[...], sa_ref[...], sb_ref[...]
    d = x_ref.shape[1]
    n_chunks = d // MXU_DIM
    for j in range(3 * n_chunks):
        y = jnp.dot(xb, w_ref[:, j * MXU_DIM:(j + 1) * MXU_DIM], preferred_element_type=F32)
        which, jj = divmod(j, n_chunks)
        if which == 2:
            v_ref[:, jj * MXU_DIM:(jj + 1) * MXU_DIM] = y.astype(BF16)
            continue
        for t in range(2):
            r = _rope(y[:, t * LANES:(t + 1) * LANES], c, sa, sb, PARTIAL_ROPE_DIM // 2)
            col = jj * MXU_DIM + t * LANES
            if which == 0:
                q_ref[:, col:col + LANES] = (r * ATTN_SCALE).astype(BF16)
            else:
                k_ref[:, col:col + LANES] = r.astype(BF16)


def _qkv_c(x2, w, tabs, seq):
    n, d = x2.shape
    tm = ROW_TILE
    c, sa, sb = tabs
    tab_spec = pl.BlockSpec((tm, LANES), lambda i: (i % (seq // tm), 0))
    row = pl.BlockSpec((tm, d), lambda i: (i, 0))
    return pl.pallas_call(
        _qkv_c_kernel,
        grid=(n // tm,),
        in_specs=[row, _const_spec((d, 3 * d)), tab_spec, tab_spec, tab_spec],
        out_specs=[row, row, row],
        out_shape=[jax.ShapeDtypeStruct((n, d), BF16)] * 3,
        compiler_params=_params(1),
        name="qkv_diff",
    )(x2, w.astype(BF16), c, sa, sb)


def _proj_ln_kernel(a_ref, w_ref, bias_ref, x_ref, g_ref, b_ref, o_ref):
    sub = jnp.dot(a_ref[...], w_ref[...], preferred_element_type=F32) + bias_ref[...]
    o_ref[...] = _layernorm(DEEPNORM_ALPHA * x_ref[...] + sub, g_ref[...], b_ref[...])


def _proj_ln(a2, w, bias, x2, g, b):
    n, d = x2.shape
    tm = ROW_TILE
    row = pl.BlockSpec((tm, d), lambda i: (i, 0))
    vec = _const_spec((1, d))
    return pl.pallas_call(
        _proj_ln_kernel,
        grid=(n // tm,),
        in_specs=[row, _const_spec((d, d)), vec, row, vec, vec],
        out_specs=row,
        out_shape=jax.ShapeDtypeStruct((n, d), F32),
        compiler_params=_params(1),
        name="proj_ln",
    )(a2, w.astype(BF16), bias[None, :], x2, g[None, :], b[None, :])


def _ffn_kernel(x_ref, xp_ref, xn_ref, wup_ref, cw_ref, cb_ref, wdn_ref, g_ref, b_ref, o_ref,
                xe_ref, h_ref, a_ref, *, tiles_per_seq):
    tm = x_ref.shape[0]
    t = pl.program_id(0) % tiles_per_seq
    x = x_ref[...]
    xe_ref[0:HALO, :] = jnp.where(t == 0, 0.0, xp_ref[...]).astype(BF16)
    xe_ref[HALO:HALO + tm, :] = x.astype(BF16)
    xe_ref[HALO + tm:2 * HALO + tm, :] = jnp.where(t == tiles_per_seq - 1, 0.0, xn_ref[...]).astype(BF16)
    n_chunks = wup_ref.shape[0]
    for ci in range(n_chunks):
        h_ref[...] = jnp.dot(xe_ref[...], wup_ref[ci], preferred_element_type=F32)
        cw = cw_ref[ci]
        hc = (cb_ref[ci]
              + h_ref[HALO - 1:HALO - 1 + tm, :] * cw[0:1]
              + h_ref[HALO:HALO + tm, :] * cw[1:2]
              + h_ref[HALO + 1:HALO + 1 + tm, :] * cw[2:3])
        gate = hc[:, :FFN_CHUNK]
        val = hc[:, FFN_CHUNK:]
        act = gate * (1.0 / (1.0 + jnp.exp(-gate))) * val
        a_ref[:, ci * FFN_CHUNK:(ci + 1) * FFN_CHUNK] = act.astype(BF16)
    sub = jnp.dot(a_ref[...], wdn_ref[...], preferred_element_type=F32)
    o_ref[...] = _layernorm(DEEPNORM_ALPHA * x + sub, g_ref[...], b_ref[...])


def _ffn(x2, w_up, conv_w, conv_b, w_down, g, b, seq):
    n, d = x2.shape
    tm = ROW_TILE
    d_ff = w_down.shape[0]
    n_chunks = d_ff // FFN_CHUNK
    def chunked(m):
        lead = m.shape[0]
        gv = m.reshape(lead, 2, n_chunks, FFN_CHUNK)
        return jnp.transpose(gv, (2, 0, 1, 3)).reshape(n_chunks, lead, 2 * FFN_CHUNK)
    wup_c = chunked(w_up.astype(BF16))
    cw_c = chunked(conv_w)
    cb_c = chunked(conv_b[None, :])
    row = pl.BlockSpec((tm, d), lambda i: (i, 0))
    hb = tm // HALO
    prev = pl.BlockSpec((HALO, d), lambda i: (jnp.maximum(i * hb - 1, 0), 0))
    nxt = pl.BlockSpec((HALO, d), lambda i: (jnp.minimum((i + 1) * hb, n // HALO - 1), 0))
    vec = _const_spec((1, d))
    return pl.pallas_call(
        functools.partial(_ffn_kernel, tiles_per_seq=seq // tm),
        grid=(n // tm,),
        in_specs=[row, prev, nxt,
                  _const_spec(wup_c.shape), _const_spec(cw_c.shape), _const_spec(cb_c.shape),
                  _const_spec((d_ff, d)), vec, vec],
        out_specs=row,
        out_shape=jax.ShapeDtypeStruct((n, d), F32),
        scratch_shapes=[pltpu.VMEM((tm + 2 * HALO, d), BF16),
                        pltpu.VMEM((tm + 2 * HALO, 2 * FFN_CHUNK), F32),
                        pltpu.VMEM((tm, d_ff), BF16)],
        compiler_params=_params(1),
        name="conv_ffn",
    )(x2, x2, x2, wup_c, cw_c, cb_c, w_down.astype(BF16), g[None, :], b[None, :])


def _dft_chan_kernel(x_ref, cs_ref, y_ref):
    gw = cs_ref.shape[0]
    for g in range(x_ref.shape[2] // gw):
        xg = x_ref[0, :, g * gw:(g + 1) * gw].astype(BF16)
        y = jnp.dot(xg, cs_ref[...], preferred_element_type=F32)
        y_ref[0, 0, :, g * gw:(g + 1) * gw] = y[:, :gw].astype(BF16)
        y_ref[0, 1, :, g * gw:(g + 1) * gw] = y[:, gw:].astype(BF16)


def _dft_seq_kernel(m_ref, y_ref, o_ref):
    o_ref[0] = jnp.dot(m_ref[...], y_ref[0], preferred_element_type=F32).astype(BF16)


def _dft_matrices(n):
    idx = jnp.arange(n, dtype=jnp.int32)
    ang = ((idx[:, None] * idx[None, :]) % n).astype(F32) * (2.0 * math.pi / n)
    scale = n ** -0.5
    return jnp.cos(ang) * scale, jnp.sin(ang) * scale


def _fourier(x):
    b, seq, d = x.shape
    gw = d // FNET_GROUPS
    tm = ROW_TILE
    cc, sc = _dft_matrices(gw)
    cs_chan = jnp.concatenate([cc, sc], axis=1).astype(BF16)
    y = pl.pallas_call(
        _dft_chan_kernel,
        grid=(b, seq // tm),
        in_specs=[pl.BlockSpec((1, tm, d), lambda bi, i: (bi, i, 0)), _const_spec((gw, 2 * gw))],
        out_specs=pl.BlockSpec((1, 2, tm, d), lambda bi, i: (bi, 0, i, 0)),
        out_shape=jax.ShapeDtypeStruct((b, 2, seq, d), BF16),
        compiler_params=_params(2),
        name="dft_channels",
    )(x, cs_chan)
    cs, ss = _dft_matrices(seq)
    m_seq = jnp.concatenate([cs, -ss], axis=1).astype(BF16)
    tn = d // 2
    return pl.pallas_call(
        _dft_seq_kernel,
        grid=(seq // tm, b, d // tn),
        in_specs=[pl.BlockSpec((tm, 2 * seq), lambda i, bi, j: (i, 0)),
                  pl.BlockSpec((1, 2 * seq, tn), lambda i, bi, j: (bi, 0, j))],
        out_specs=pl.BlockSpec((1, tm, tn), lambda i, bi, j: (bi, i, j)),
        out_shape=jax.ShapeDtypeStruct((b, seq, d), BF16),
        compiler_params=_params(3),
        name="dft_sequence",
    )(m_seq, y.reshape(b, 2 * seq, d))


def _cos_sin(pos, dim, theta):
    inv_freq = theta ** (-jnp.arange(0, dim, 2, dtype=F32) / dim)
    ang = pos.astype(F32)[:, None] * inv_freq[None, :]
    return jnp.cos(ang), jnp.sin(ang)


def _axial_tables(seq):
    rows = seq // GRID_W
    t_row = jnp.repeat(jnp.arange(rows, dtype=jnp.int32), GRID_W)
    t_col = jnp.tile(jnp.arange(GRID_W, dtype=jnp.int32), rows)
    cr, sr = _cos_sin(t_row, HEAD_DIM // 2, AXIAL_THETA)
    cc, sc = _cos_sin(t_col, HEAD_DIM // 2, AXIAL_THETA)
    z = jnp.zeros_like(sr)
    c = jnp.concatenate([cr, cr, cc, cc], axis=1)
    sa = jnp.concatenate([-sr, z, -sc, z], axis=1)
    sb = jnp.concatenate([z, sr, z, sc], axis=1)
    return tuple(jnp.tile(t, (1, LANES // HEAD_DIM)) for t in (c, sa, sb))


def _partial_tables(seq):
    co, si = _cos_sin(jnp.arange(seq, dtype=jnp.int32), PARTIAL_ROPE_DIM, ROPE_THETA)
    rest = HEAD_DIM - PARTIAL_ROPE_DIM
    z = jnp.zeros_like(si)
    zr = jnp.zeros((seq, rest), F32)
    c = jnp.concatenate([co, co, jnp.ones((seq, rest), F32)], axis=1)
    sa = jnp.concatenate([-si, z, zr], axis=1)
    sb = jnp.concatenate([z, si, zr], axis=1)
    return tuple(jnp.tile(t, (1, LANES // HEAD_DIM)) for t in (c, sa, sb))


def _lambda_init(layer_idx):
    return 0.8 - 0.6 * math.exp(-0.3 * layer_idx)


def kernel(x, l0_a_wqkv, l0_a_qnorm, l0_a_knorm, l0_a_wo, l0_ln1_g, l0_ln1_b, l0_ffn_wup, l0_ffn_conv_w, l0_ffn_conv_b, l0_ffn_wdown, l0_ln2_g, l0_ln2_b, l1_f_wo, l1_f_bo, l1_ln1_g, l1_ln1_b, l1_ffn_wup, l1_ffn_conv_w, l1_ffn_conv_b, l1_ffn_wdown, l1_ln2_g, l1_ln2_b, l2_c_wqkv, l2_c_lq1, l2_c_lk1, l2_c_lq2, l2_c_lk2, l2_c_subln, l2_c_wo, l2_ln1_g, l2_ln1_b, l2_ffn_wup, l2_ffn_conv_w, l2_ffn_conv_b, l2_ffn_wdown, l2_ln2_g, l2_ln2_b, l3_a_wqkv, l3_a_qnorm, l3_a_knorm, l3_a_wo, l3_ln1_g, l3_ln1_b, l3_ffn_wup, l3_ffn_conv_w, l3_ffn_conv_b, l3_ffn_wdown, l3_ln2_g, l3_ln2_b):
    b, seq, d = x.shape
    n = b * seq
    assert seq % ROW_TILE == 0 and seq % KV_CHUNK == 0 and d == A_HEADS * HEAD_DIM
    axial = _axial_tables(seq)
    partial_t = _partial_tables(seq)
    zero_bias = jnp.zeros((d,), F32)

    def gqa_layer(x2, wqkv, qn, kn, wo, g, bb):
        q, kd, vd = _qkv_a(x2, wqkv, qn, kn, axial, seq)
        o = _attn_a(q.reshape(b, seq, d), kd.reshape(b, seq, -1), vd.reshape(b, seq, -1))
        return _proj_ln(o.reshape(n, d), wo, zero_bias, x2, g, bb)

    x2 = x.reshape(n, d)
    x2 = gqa_layer(x2, l0_a_wqkv, l0_a_qnorm, l0_a_knorm, l0_a_wo, l0_ln1_g, l0_ln1_b)
    x2 = _ffn(x2, l0_ffn_wup, l0_ffn_conv_w, l0_ffn_conv_b, l0_ffn_wdown, l0_ln2_g, l0_ln2_b, seq)
    f = _fourier(x2.reshape(b, seq, d))
    x2 = _proj_ln(f.reshape(n, d), l1_f_wo, l1_f_bo, x2, l1_ln1_g, l1_ln1_b)
    x2 = _ffn(x2, l1_ffn_wup, l1_ffn_conv_w, l1_ffn_conv_b, l1_ffn_wdown, l1_ln2_g, l1_ln2_b, seq)
    q, k, v = _qkv_c(x2, l2_c_wqkv, partial_t, seq)
    lqk = jnp.stack([l2_c_lq1, l2_c_lk1, l2_c_lq2, l2_c_lk2]).astype(F32)
    o = _attn_c(q.reshape(b, seq, d), k.reshape(b, seq, d), v.reshape(b, seq, d), lqk, l2_c_subln,
                _lambda_init(2))
    x2 = _proj_ln(o.reshape(n, d), l2_c_wo, zero_bias, x2, l2_ln1_g, l2_ln1_b)
    x2 = _ffn(x2, l2_ffn_wup, l2_ffn_conv_w, l2_ffn_conv_b, l2_ffn_wdown, l2_ln2_g, l2_ln2_b, seq)
    x2 = gqa_layer(x2, l3_a_wqkv, l3_a_qnorm, l3_a_knorm, l3_a_wo, l3_ln1_g, l3_ln1_b)
    x2 = _ffn(x2, l3_ffn_wup, l3_ffn_conv_w, l3_ffn_conv_b, l3_ffn_wdown, l3_ln2_g, l3_ln2_b, seq)
    return x2.reshape(b, seq, d)
```

```python
import functools
import math

import jax
import jax.numpy as jnp
from jax import lax
from jax.experimental import pallas as pl
from jax.experimental.pallas import tpu as pltpu

F32 = jnp.float32
BF16 = jnp.bfloat16

DEPTH = 4
GRID_W = 64
A_HEADS = 16
A_KV_HEADS = 4
HEAD_DIM = 64
AXIAL_THETA = 10000.0
FNET_GROUPS = 4
C_HEADS = 8
PARTIAL_ROPE_DIM = 16
ROPE_THETA = 500000.0
CONV_WIDTH = 3
DEEPNORM_ALPHA = (2.0 * DEPTH) ** 0.25
LN_EPS = 1e-5
RMS_EPS = 1e-6
ATTN_SCALE = HEAD_DIM ** -0.5 * math.log2(math.e)

LANES = 128
MXU_DIM = 256
BF16_SUBLANES = 16
VMEM_LIMIT = 56 * 1024 * 1024

ROW_TILE = 512
FFN_CHUNK = 256
HALO = BF16_SUBLANES
STREAM_ROWS = 256


def _const_spec(shape):
    return pl.BlockSpec(shape, lambda *_: (0,) * len(shape), pipeline_mode=pl.Buffered(1))


def _params(n_axes):
    return pltpu.CompilerParams(dimension_semantics=("arbitrary",) * n_axes,
                                vmem_limit_bytes=VMEM_LIMIT)


def _layernorm(y, g, b):
    mu = jnp.mean(y, axis=-1, keepdims=True)
    d = y - mu
    var = jnp.mean(d * d, axis=-1, keepdims=True)
    return d * lax.rsqrt(var + LN_EPS) * g + b


def _rope(y, c, sa, sb, d):
    return y * c + pltpu.roll(y, LANES - d, 1) * sa + pltpu.roll(y, d, 1) * sb


def _dup_halves(blk):
    swapped = pltpu.roll(blk, HEAD_DIM, 1)
    lo = lax.broadcasted_iota(jnp.int32, blk.shape, 1) < HEAD_DIM
    return jnp.where(lo, blk, swapped), jnp.where(lo, swapped, blk)


def _qkv_a_kernel(x_ref, w_ref, bd_ref, gq_ref, gk_ref, c_ref, sa_ref, sb_ref, q_ref, k_ref, v_ref):
    xb = x_ref[...].astype(BF16)
    c, sa, sb = c_ref[...], sa_ref[...], sb_ref[...]
    bd = bd_ref[...]
    n_q = A_HEADS * HEAD_DIM // MXU_DIM

    def head_norm(y, g):
        sq = y * y
        hi = sq.astype(BF16)
        lo = (sq - hi.astype(F32)).astype(BF16)
        ms = (jnp.dot(hi, bd, preferred_element_type=F32) + jnp.dot(lo, bd, preferred_element_type=F32))
        return y * lax.rsqrt(ms + RMS_EPS) * g

    for j in range(n_q + 2):
        y = jnp.dot(xb, w_ref[:, j * MXU_DIM:(j + 1) * MXU_DIM], preferred_element_type=F32)
        if j < n_q:
            yn = head_norm(y, gq_ref[...])
            for t in range(2):
                r = _rope(yn[:, t * LANES:(t + 1) * LANES], c, sa, sb, HEAD_DIM // 4)
                q_ref[:, j * MXU_DIM + t * LANES:j * MXU_DIM + (t + 1) * LANES] = (r * ATTN_SCALE).astype(BF16)
        elif j == n_q:
            yn = head_norm(y, gk_ref[...])
            for t in range(2):
                r = _rope(yn[:, t * LANES:(t + 1) * LANES], c, sa, sb, HEAD_DIM // 4)
                d0, d1 = _dup_halves(r)
                k_ref[:, (2 * t) * LANES:(2 * t + 1) * LANES] = d0.astype(BF16)
                k_ref[:, (2 * t + 1) * LANES:(2 * t + 2) * LANES] = d1.astype(BF16)
        else:
            for t in range(2):
                d0, d1 = _dup_halves(y[:, t * LANES:(t + 1) * LANES])
                v_ref[:, (2 * t) * LANES:(2 * t + 1) * LANES] = d0.astype(BF16)
                v_ref[:, (2 * t + 1) * LANES:(2 * t + 2) * LANES] = d1.astype(BF16)


def _qkv_a(x2, w, gq, gk, tabs, seq):
    n, d = x2.shape
    tm = ROW_TILE
    n_out = w.shape[1]
    c, sa, sb = tabs
    bd = (jnp.arange(MXU_DIM)[:, None] // HEAD_DIM == jnp.arange(MXU_DIM)[None, :] // HEAD_DIM)
    bd = (bd.astype(F32) / HEAD_DIM).astype(BF16)
    gq2 = jnp.tile(gq, MXU_DIM // HEAD_DIM)[None, :]
    gk2 = jnp.tile(gk, MXU_DIM // HEAD_DIM)[None, :]
    tab_spec = pl.BlockSpec((tm, LANES), lambda i: (i % (seq // tm), 0))
    kvw = A_KV_HEADS * LANES
    return pl.pallas_call(
        _qkv_a_kernel,
        grid=(n // tm,),
        in_specs=[pl.BlockSpec((tm, d), lambda i: (i, 0)),
                  _const_spec((d, n_out)), _const_spec((MXU_DIM, MXU_DIM)),
                  _const_spec((1, MXU_DIM)), _const_spec((1, MXU_DIM)),
                  tab_spec, tab_spec, tab_spec],
        out_specs=[pl.BlockSpec((tm, d), lambda i: (i, 0)),
                   pl.BlockSpec((tm, kvw), lambda i: (i, 0)),
                   pl.BlockSpec((tm, kvw), lambda i: (i, 0))],
        out_shape=[jax.ShapeDtypeStruct((n, d), BF16),
                   jax.ShapeDtypeStruct((n, kvw), BF16),
                   jax.ShapeDtypeStruct((n, kvw), BF16)],
        compiler_params=_params(1),
        name="qkv_gqa",
    )(x2, w.astype(BF16), bd, gq2, gk2, c, sa, sb)


def _attend_streams(q_list, k_ref, v_ref):
    k = k_ref[0]
    v = v_ref[0]

    def scores(q):
        return lax.dot_general(q, k, (((1,), (1,)), ((), ())), preferred_element_type=F32)

    def finish(s):
        m = jnp.max(s, axis=-1, keepdims=True)
        p = jnp.exp2(s - m)
        l = jnp.sum(p, axis=-1, keepdims=True)
        return jnp.dot(p.astype(BF16), v, preferred_element_type=F32) / l

    outs = []
    s_next = scores(q_list[0])
    for i in range(len(q_list)):
        s_cur = s_next
        if i + 1 < len(q_list):
            s_next = scores(q_list[i + 1])
        outs.append(finish(s_cur))
    return outs


def _attn_a_kernel(q_ref, k_ref, v_ref, o_ref, *, tq):
    lo = lax.broadcasted_iota(jnp.int32, (tq, LANES), 1) < HEAD_DIM
    streams = []
    for h in range(4):
        qp = q_ref[0, :, (h // 2) * LANES:(h // 2 + 1) * LANES].astype(F32)
        keep = lo if h % 2 == 0 else jnp.logical_not(lo)
        streams.append(jnp.where(keep, qp, 0.0).astype(BF16))
    o = _attend_streams(streams, k_ref, v_ref)
    o_ref[0, :, 0:LANES] = jnp.where(lo, o[0], o[1]).astype(BF16)
    o_ref[0, :, LANES:2 * LANES] = jnp.where(lo, o[2], o[3]).astype(BF16)


def _attn_a(q, kd, vd):
    b, seq, d = q.shape
    tq = STREAM_ROWS
    return pl.pallas_call(
        functools.partial(_attn_a_kernel, tq=tq),
        grid=(b, A_KV_HEADS, seq // tq),
        in_specs=[pl.BlockSpec((1, tq, MXU_DIM), lambda bi, g, i: (bi, i, g)),
                  pl.BlockSpec((1, seq, LANES), lambda bi, g, i: (bi, 0, g)),
                  pl.BlockSpec((1, seq, LANES), lambda bi, g, i: (bi, 0, g))],
        out_specs=pl.BlockSpec((1, tq, MXU_DIM), lambda bi, g, i: (bi, i, g)),
        out_shape=jax.ShapeDtypeStruct((b, seq, d), BF16),
        compiler_params=_params(3),
        name="attn_gqa",
    )(q, kd, vd)


def _attn_c_kernel(lqk_ref, subln_ref, q_ref, k_ref, v_ref, o_ref, *, tq, lambda_init):
    lqk = lqk_ref[...]
    lam = (jnp.exp(jnp.sum(lqk[0:1] * lqk[1:2], axis=-1, keepdims=True))
           - jnp.exp(jnp.sum(lqk[2:3] * lqk[3:4], axis=-1, keepdims=True)) + lambda_init)
    lo = lax.broadcasted_iota(jnp.int32, (STREAM_ROWS, LANES), 1) < HEAD_DIM
    streams = []
    for r in range(tq // STREAM_ROWS):
        qp = q_ref[0, r * STREAM_ROWS:(r + 1) * STREAM_ROWS, :].astype(F32)
        streams.append(jnp.where(lo, qp, 0.0).astype(BF16))
        streams.append(jnp.where(lo, 0.0, qp).astype(BF16))
    outs = _attend_streams(streams, k_ref, v_ref)
    for r in range(tq // STREAM_ROWS):
        o = outs[2 * r] - lam * outs[2 * r + 1]
        ms = jnp.mean(o * o, axis=-1, keepdims=True)
        o = o * lax.rsqrt(ms + RMS_EPS) * subln_ref[...] * (1.0 - lambda_init)
        o_ref[0, r * STREAM_ROWS:(r + 1) * STREAM_ROWS, :] = o.astype(BF16)


def _attn_c(q, k, v, lqk, subln, lambda_init):
    b, seq, d = q.shape
    tq = 2 * STREAM_ROWS
    blk = lambda bi, h, i: (bi, i, h)
    kv = lambda bi, h, i: (bi, 0, h)
    return pl.pallas_call(
        functools.partial(_attn_c_kernel, tq=tq, lambda_init=lambda_init),
        grid=(b, C_HEADS, seq // tq),
        in_specs=[_const_spec((4, HEAD_DIM)), _const_spec((1, LANES)),
                  pl.BlockSpec((1, tq, LANES), blk),
                  pl.BlockSpec((1, seq, LANES), kv),
                  pl.BlockSpec((1, seq, LANES), kv)],
        out_specs=pl.BlockSpec((1, tq, LANES), blk),
        out_shape=jax.ShapeDtypeStruct((b, seq, d), BF16),
        compiler_params=_params(3),
        name="attn_diff",
    )(lqk, subln[None, :], q, k, v)


def _qkv_c_kernel(x_ref, w_ref, c_ref, sa_ref, sb_ref, q_ref, k_ref, v_ref):
    xb = x_ref[...].astype(BF16)
    c, sa, sb = c_ref[...], sa_ref[...], sb_ref[...]
    d = x_ref.shape[1]
    n_chunks = d // MXU_DIM
    for j in range(3 * n_chunks):
        y = jnp.dot(xb, w_ref[:, j * MXU_DIM:(j + 1) * MXU_DIM], preferred_element_type=F32)
        which, jj = divmod(j, n_chunks)
        if which == 2:
            v_ref[:, jj * MXU_DIM:(jj + 1) * MXU_DIM] = y.astype(BF16)
            continue
        for t in range(2):
            r = _rope(y[:, t * LANES:(t + 1) * LANES], c, sa, sb, PARTIAL_ROPE_DIM // 2)
            col = jj * MXU_DIM + t * LANES
            if which == 0:
                q_ref[:, col:col + LANES] = (r * ATTN_SCALE).astype(BF16)
            else:
                k_ref[:, col:col + LANES] = r.astype(BF16)


def _qkv_c(x2, w, tabs, seq):
    n, d = x2.shape
    tm = ROW_TILE
    c, sa, sb = tabs
    tab_spec = pl.BlockSpec((tm, LANES), lambda i: (i % (seq // tm), 0))
    row = pl.BlockSpec((tm, d), lambda i: (i, 0))
    return pl.pallas_call(
        _qkv_c_kernel,
        grid=(n // tm,),
        in_specs=[row, _const_spec((d, 3 * d)), tab_spec, tab_spec, tab_spec],
        out_specs=[row, row, row],
        out_shape=[jax.ShapeDtypeStruct((n, d), BF16)] * 3,
        compiler_params=_params(1),
        name="qkv_diff",
    )(x2, w.astype(BF16), c, sa, sb)


def _proj_ln_kernel(a_ref, w_ref, bias_ref, x_ref, g_ref, b_ref, o_ref):
    sub = jnp.dot(a_ref[...], w_ref[...], preferred_element_type=F32) + bias_ref[...]
    o_ref[...] = _layernorm(DEEPNORM_ALPHA * x_ref[...] + sub, g_ref[...], b_ref[...])


def _proj_ln(a2, w, bias, x2, g, b):
    n, d = x2.shape
    tm = ROW_TILE
    row = pl.BlockSpec((tm, d), lambda i: (i, 0))
    vec = _const_spec((1, d))
    return pl.pallas_call(
        _proj_ln_kernel,
        grid=(n // tm,),
        in_specs=[row, _const_spec((d, d)), vec, row, vec, vec],
        out_specs=row,
        out_shape=jax.ShapeDtypeStruct((n, d), F32),
        compiler_params=_params(1),
        name="proj_ln",
    )(a2, w.astype(BF16), bias[None, :], x2, g[None, :], b[None, :])


def _ffn_kernel(x_ref, xp_ref, xn_ref, wup_ref, cw_ref, cb_ref, wdn_ref, g_ref, b_ref, o_ref,
                xe_ref, h_ref, a_ref, *, tiles_per_seq):
    tm = x_ref.shape[0]
    t = pl.program_id(0) % tiles_per_seq
    x = x_ref[...]
    xe_ref[0:HALO, :] = jnp.where(t == 0, 0.0, xp_ref[...]).astype(BF16)
    xe_ref[HALO:HALO + tm, :] = x.astype(BF16)
    xe_ref[HALO + tm:2 * HALO + tm, :] = jnp.where(t == tiles_per_seq - 1, 0.0, xn_ref[...]).astype(BF16)
    n_chunks = wup_ref.shape[0]
    for ci in range(n_chunks):
        h_ref[...] = jnp.dot(xe_ref[...], wup_ref[ci], preferred_element_type=F32)
        cw = cw_ref[ci]
        hc = (cb_ref[ci]
              + h_ref[HALO - 1:HALO - 1 + tm, :] * cw[0:1]
              + h_ref[HALO:HALO + tm, :] * cw[1:2]
              + h_ref[HALO + 1:HALO + 1 + tm, :] * cw[2:3])
        gate = hc[:, :FFN_CHUNK]
        val = hc[:, FFN_CHUNK:]
        act = gate * (1.0 / (1.0 + jnp.exp(-gate))) * val
        a_ref[:, ci * FFN_CHUNK:(ci + 1) * FFN_CHUNK] = act.astype(BF16)
    sub = jnp.dot(a_ref[...], wdn_ref[...], preferred_element_type=F32)
    o_ref[...] = _layernorm(DEEPNORM_ALPHA * x + sub, g_ref[...], b_ref[...])


def _ffn(x2, w_up, conv_w, conv_b, w_down, g, b, seq):
    n, d = x2.shape
    tm = ROW_TILE
    d_ff = w_down.shape[0]
    n_chunks = d_ff // FFN_CHUNK
    def chunked(m):
        lead = m.shape[0]
        gv = m.reshape(lead, 2, n_chunks, FFN_CHUNK)
        return jnp.transpose(gv, (2, 0, 1, 3)).reshape(n_chunks, lead, 2 * FFN_CHUNK)
    wup_c = chunked(w_up.astype(BF16))
    cw_c = chunked(conv_w)
    cb_c = chunked(conv_b[None, :])
    row = pl.BlockSpec((tm, d), lambda i: (i, 0))
    hb = tm // HALO
    prev = pl.BlockSpec((HALO, d), lambda i: (jnp.maximum(i * hb - 1, 0), 0))
    nxt = pl.BlockSpec((HALO, d), lambda i: (jnp.minimum((i + 1) * hb, n // HALO - 1), 0))
    vec = _const_spec((1, d))
    return pl.pallas_call(
        functools.partial(_ffn_kernel, tiles_per_seq=seq // tm),
        grid=(n // tm,),
        in_specs=[row, prev, nxt,
                  _const_spec(wup_c.shape), _const_spec(cw_c.shape), _const_spec(cb_c.shape),
                  _const_spec((d_ff, d)), vec, vec],
        out_specs=row,
        out_shape=jax.ShapeDtypeStruct((n, d), F32),
        scratch_shapes=[pltpu.VMEM((tm + 2 * HALO, d), BF16),
                        pltpu.VMEM((tm + 2 * HALO, 2 * FFN_CHUNK), F32),
                        pltpu.VMEM((tm, d_ff), BF16)],
        compiler_params=_params(1),
        name="conv_ffn",
    )(x2, x2, x2, wup_c, cw_c, cb_c, w_down.astype(BF16), g[None, :], b[None, :])


def _dft_chan_kernel(x_ref, cs_ref, y_ref):
    gw = cs_ref.shape[0]
    for g in range(x_ref.shape[2] // gw):
        xg = x_ref[0, :, g * gw:(g + 1) * gw].astype(BF16)
        y = jnp.dot(xg, cs_ref[...], preferred_element_type=F32)
        y_ref[0, 0, :, g * gw:(g + 1) * gw] = y[:, :gw].astype(BF16)
        y_ref[0, 1, :, g * gw:(g + 1) * gw] = y[:, gw:].astype(BF16)


def _dft_seq_kernel(m_ref, y_ref, o_ref):
    o_ref[0] = jnp.dot(m_ref[...], y_ref[0], preferred_element_type=F32).astype(BF16)


def _dft_matrices(n):
    idx = jnp.arange(n, dtype=jnp.int32)
    ang = ((idx[:, None] * idx[None, :]) % n).astype(F32) * (2.0 * math.pi / n)
    scale = n ** -0.5
    return jnp.cos(ang) * scale, jnp.sin(ang) * scale


def _fourier(x):
    b, seq, d = x.shape
    gw = d // FNET_GROUPS
    tm = ROW_TILE
    cc, sc = _dft_matrices(gw)
    cs_chan = jnp.concatenate([cc, sc], axis=1).astype(BF16)
    y = pl.pallas_call(
        _dft_chan_kernel,
        grid=(b, seq // tm),
        in_specs=[pl.BlockSpec((1, tm, d), lambda bi, i: (bi, i, 0)), _const_spec((gw, 2 * gw))],
        out_specs=pl.BlockSpec((1, 2, tm, d), lambda bi, i: (bi, 0, i, 0)),
        out_shape=jax.ShapeDtypeStruct((b, 2, seq, d), BF16),
        compiler_params=_params(2),
        name="dft_channels",
    )(x, cs_chan)
    cs, ss = _dft_matrices(seq)
    m_seq = jnp.concatenate([cs, -ss], axis=1).astype(BF16)
    tn = d // 2
    return pl.pallas_call(
        _dft_seq_kernel,
        grid=(seq // tm, b, d // tn),
        in_specs=[pl.BlockSpec((tm, 2 * seq), lambda i, bi, j: (i, 0)),
                  pl.BlockSpec((1, 2 * seq, tn), lambda i, bi, j: (bi, 0, j))],
        out_specs=pl.BlockSpec((1, tm, tn), lambda i, bi, j: (bi, i, j)),
        out_shape=jax.ShapeDtypeStruct((b, seq, d), BF16),
        compiler_params=_params(3),
        name="dft_sequence",
    )(m_seq, y.reshape(b, 2 * seq, d))


def _cos_sin(pos, dim, theta):
    inv_freq = theta ** (-jnp.arange(0, dim, 2, dtype=F32) / dim)
    ang = pos.astype(F32)[:, None] * inv_freq[None, :]
    return jnp.cos(ang), jnp.sin(ang)


def _axial_tables(seq):
    rows = seq // GRID_W
    t_row = jnp.repeat(jnp.arange(rows, dtype=jnp.int32), GRID_W)
    t_col = jnp.tile(jnp.arange(GRID_W, dtype=jnp.int32), rows)
    cr, sr = _cos_sin(t_row, HEAD_DIM // 2, AXIAL_THETA)
    cc, sc = _cos_sin(t_col, HEAD_DIM // 2, AXIAL_THETA)
    z = jnp.zeros_like(sr)
    c = jnp.concatenate([cr, cr, cc, cc], axis=1)
    sa = jnp.concatenate([-sr, z, -sc, z], axis=1)
    sb = jnp.concatenate([z, sr, z, sc], axis=1)
    return tuple(jnp.tile(t, (1, LANES // HEAD_DIM)) for t in (c, sa, sb))


def _partial_tables(seq):
    co, si = _cos_sin(jnp.arange(seq, dtype=jnp.int32), PARTIAL_ROPE_DIM, ROPE_THETA)
    rest = HEAD_DIM - PARTIAL_ROPE_DIM
    z = jnp.zeros_like(si)
    zr = jnp.zeros((seq, rest), F32)
    c = jnp.concatenate([co, co, jnp.ones((seq, rest), F32)], axis=1)
    sa = jnp.concatenate([-si, z, zr], axis=1)
    sb = jnp.concatenate([z, si, zr], axis=1)
    return tuple(jnp.tile(t, (1, LANES // HEAD_DIM)) for t in (c, sa, sb))


def _lambda_init(layer_idx):
    return 0.8 - 0.6 * math.exp(-0.3 * layer_idx)


def kernel(x, l0_a_wqkv, l0_a_qnorm, l0_a_knorm, l0_a_wo, l0_ln1_g, l0_ln1_b, l0_ffn_wup, l0_ffn_conv_w, l0_ffn_conv_b, l0_ffn_wdown, l0_ln2_g, l0_ln2_b, l1_f_wo, l1_f_bo, l1_ln1_g, l1_ln1_b, l1_ffn_wup, l1_ffn_conv_w, l1_ffn_conv_b, l1_ffn_wdown, l1_ln2_g, l1_ln2_b, l2_c_wqkv, l2_c_lq1, l2_c_lk1, l2_c_lq2, l2_c_lk2, l2_c_subln, l2_c_wo, l2_ln1_g, l2_ln1_b, l2_ffn_wup, l2_ffn_conv_w, l2_ffn_conv_b, l2_ffn_wdown, l2_ln2_g, l2_ln2_b, l3_a_wqkv, l3_a_qnorm, l3_a_knorm, l3_a_wo, l3_ln1_g, l3_ln1_b, l3_ffn_wup, l3_ffn_conv_w, l3_ffn_conv_b, l3_ffn_wdown, l3_ln2_g, l3_ln2_b):
    b, seq, d = x.shape
    n = b * seq
    assert seq % ROW_TILE == 0 and seq % (2 * STREAM_ROWS) == 0 and d == A_HEADS * HEAD_DIM
    axial = _axial_tables(seq)
    partial_t = _partial_tables(seq)
    zero_bias = jnp.zeros((d,), F32)

    def gqa_layer(x2, wqkv, qn, kn, wo, g, bb):
        q, kd, vd = _qkv_a(x2, wqkv, qn, kn, axial, seq)
        o = _attn_a(q.reshape(b, seq, d), kd.reshape(b, seq, -1), vd.reshape(b, seq, -1))
        return _proj_ln(o.reshape(n, d), wo, zero_bias, x2, g, bb)

    x2 = x.reshape(n, d)
    x2 = gqa_layer(x2, l0_a_wqkv, l0_a_qnorm, l0_a_knorm, l0_a_wo, l0_ln1_g, l0_ln1_b)
    x2 = _ffn(x2, l0_ffn_wup, l0_ffn_conv_w, l0_ffn_conv_b, l0_ffn_wdown, l0_ln2_g, l0_ln2_b, seq)
    f = _fourier(x2.reshape(b, seq, d))
    x2 = _proj_ln(f.reshape(n, d), l1_f_wo, l1_f_bo, x2, l1_ln1_g, l1_ln1_b)
    x2 = _ffn(x2, l1_ffn_wup, l1_ffn_conv_w, l1_ffn_conv_b, l1_ffn_wdown, l1_ln2_g, l1_ln2_b, seq)
    q, k, v = _qkv_c(x2, l2_c_wqkv, partial_t, seq)
    lqk = jnp.stack([l2_c_lq1, l2_c_lk1, l2_c_lq2, l2_c_lk2]).astype(F32)
    o = _attn_c(q.reshape(b, seq, d), k.reshape(b, seq, d), v.reshape(b, seq, d), lqk, l2_c_subln,
                _lambda_init(2))
    x2 = _proj_ln(o.reshape(n, d), l2_c_wo, zero_bias, x2, l2_ln1_g, l2_ln1_b)
    x2 = _ffn(x2, l2_ffn_wup, l2_ffn_conv_w, l2_ffn_conv_b, l2_ffn_wdown, l2_ln2_g, l2_ln2_b, seq)
    x2 = gqa_layer(x2, l3_a_wqkv, l3_a_qnorm, l3_a_knorm, l3_a_wo, l3_ln1_g, l3_ln1_b)
    x2 = _ffn(x2, l3_ffn_wup, l3_ffn_conv_w, l3_ffn_conv_b, l3_ffn_wdown, l3_ln2_g, l3_ln2_b, seq)
    return x2.reshape(b, seq, d)
```

```python
import functools
import math

import jax
import jax.numpy as jnp
from jax import lax
from jax.experimental import pallas as pl
from jax.experimental.pallas import tpu as pltpu

F32 = jnp.float32
BF16 = jnp.bfloat16

DEPTH = 4
GRID_W = 64
A_HEADS = 16
A_KV_HEADS = 4
HEAD_DIM = 64
AXIAL_THETA = 10000.0
FNET_GROUPS = 4
C_HEADS = 8
PARTIAL_ROPE_DIM = 16
ROPE_THETA = 500000.0
CONV_WIDTH = 3
DEEPNORM_ALPHA = (2.0 * DEPTH) ** 0.25
LN_EPS = 1e-5
RMS_EPS = 1e-6
ATTN_SCALE = HEAD_DIM ** -0.5 * math.log2(math.e)

LANES = 128
MXU_DIM = 256
BF16_SUBLANES = 16
VMEM_LIMIT = 56 * 1024 * 1024

ROW_TILE = 512
FFN_CHUNK = 256
HALO = BF16_SUBLANES
STREAM_ROWS = 256
KEY_TILE = MXU_DIM
STREAMS_PER_STEP = 8


def _const_spec(shape):
    return pl.BlockSpec(shape, lambda *_: (0,) * len(shape), pipeline_mode=pl.Buffered(1))


def _params(n_axes):
    return pltpu.CompilerParams(dimension_semantics=("arbitrary",) * n_axes,
                                vmem_limit_bytes=VMEM_LIMIT)


def _layernorm(y, g, b):
    mu = jnp.mean(y, axis=-1, keepdims=True)
    d = y - mu
    var = jnp.mean(d * d, axis=-1, keepdims=True)
    return d * lax.rsqrt(var + LN_EPS) * g + b


def _rope(y, c, sa, sb, d):
    return y * c + pltpu.roll(y, LANES - d, 1) * sa + pltpu.roll(y, d, 1) * sb


def _dup_halves(blk):
    swapped = pltpu.roll(blk, HEAD_DIM, 1)
    lo = lax.broadcasted_iota(jnp.int32, blk.shape, 1) < HEAD_DIM
    return jnp.where(lo, blk, swapped), jnp.where(lo, swapped, blk)


def _qkv_a_kernel(x_ref, w_ref, bd_ref, gq_ref, gk_ref, c_ref, sa_ref, sb_ref, q_ref, k_ref, v_ref):
    xb = x_ref[...].astype(BF16)
    c, sa, sb = c_ref[...], sa_ref[...], sb_ref[...]
    bd = bd_ref[...]
    n_q = A_HEADS * HEAD_DIM // MXU_DIM

    def head_norm(y, g):
        sq = y * y
        hi = sq.astype(BF16)
        lo = (sq - hi.astype(F32)).astype(BF16)
        ms = (jnp.dot(hi, bd, preferred_element_type=F32) + jnp.dot(lo, bd, preferred_element_type=F32))
        return y * lax.rsqrt(ms + RMS_EPS) * g

    for j in range(n_q + 2):
        y = jnp.dot(xb, w_ref[:, j * MXU_DIM:(j + 1) * MXU_DIM], preferred_element_type=F32)
        if j < n_q:
            yn = head_norm(y, gq_ref[...])
            for t in range(2):
                r = _rope(yn[:, t * LANES:(t + 1) * LANES], c, sa, sb, HEAD_DIM // 4)
                q_ref[:, j * MXU_DIM + t * LANES:j * MXU_DIM + (t + 1) * LANES] = (r * ATTN_SCALE).astype(BF16)
        elif j == n_q:
            yn = head_norm(y, gk_ref[...])
            for t in range(2):
                r = _rope(yn[:, t * LANES:(t + 1) * LANES], c, sa, sb, HEAD_DIM // 4)
                d0, d1 = _dup_halves(r)
                k_ref[:, (2 * t) * LANES:(2 * t + 1) * LANES] = d0.astype(BF16)
                k_ref[:, (2 * t + 1) * LANES:(2 * t + 2) * LANES] = d1.astype(BF16)
        else:
            lo_half = lax.broadcasted_iota(jnp.int32, (y.shape[0], LANES), 1) < HEAD_DIM
            for t in range(2):
                for u, dup in enumerate(_dup_halves(y[:, t * LANES:(t + 1) * LANES])):
                    col = (2 * t + u) * MXU_DIM
                    v_ref[:, col:col + LANES] = jnp.where(lo_half, dup, 1.0).astype(BF16)
                    v_ref[:, col + LANES:col + 2 * LANES] = jnp.where(lo_half, 1.0, dup).astype(BF16)


def _qkv_a(x2, w, gq, gk, tabs, seq):
    n, d = x2.shape
    tm = ROW_TILE
    n_out = w.shape[1]
    c, sa, sb = tabs
    bd = (jnp.arange(MXU_DIM)[:, None] // HEAD_DIM == jnp.arange(MXU_DIM)[None, :] // HEAD_DIM)
    bd = (bd.astype(F32) / HEAD_DIM).astype(BF16)
    gq2 = jnp.tile(gq, MXU_DIM // HEAD_DIM)[None, :]
    gk2 = jnp.tile(gk, MXU_DIM // HEAD_DIM)[None, :]
    tab_spec = pl.BlockSpec((tm, LANES), lambda i: (i % (seq // tm), 0))
    kvw = A_KV_HEADS * LANES
    return pl.pallas_call(
        _qkv_a_kernel,
        grid=(n // tm,),
        in_specs=[pl.BlockSpec((tm, d), lambda i: (i, 0)),
                  _const_spec((d, n_out)), _const_spec((MXU_DIM, MXU_DIM)),
                  _const_spec((1, MXU_DIM)), _const_spec((1, MXU_DIM)),
                  tab_spec, tab_spec, tab_spec],
        out_specs=[pl.BlockSpec((tm, d), lambda i: (i, 0)),
                   pl.BlockSpec((tm, kvw), lambda i: (i, 0)),
                   pl.BlockSpec((tm, 2 * kvw), lambda i: (i, 0))],
        out_shape=[jax.ShapeDtypeStruct((n, d), BF16),
                   jax.ShapeDtypeStruct((n, kvw), BF16),
                   jax.ShapeDtypeStruct((n, 2 * kvw), BF16)],
        compiler_params=_params(1),
        name="qkv_gqa",
    )(x2, w.astype(BF16), bd, gq2, gk2, c, sa, sb)


def _attend_streams(q_list, k_ref, v_ref, v_cols, s_ref, emit):
    rows, seq = s_ref.shape[1], s_ref.shape[2]
    k = k_ref[0]

    def scores(i):
        s_ref[i % 2] = lax.dot_general(q_list[i], k, (((1,), (1,)), ((), ())), preferred_element_type=F32)

    def finish(i):
        sb = s_ref.at[i % 2]
        c0, cw = v_cols[i]
        m = jnp.broadcast_to(jnp.max(sb[...], axis=-1, keepdims=True), (rows, KEY_TILE))
        acc = None
        for kt in range(seq // KEY_TILE):
            p = jnp.exp2(sb[:, kt * KEY_TILE:(kt + 1) * KEY_TILE] - m)
            part = jnp.dot(p.astype(BF16), v_ref[0, kt * KEY_TILE:(kt + 1) * KEY_TILE, c0:c0 + cw],
                           preferred_element_type=F32)
            acc = part if acc is None else acc + part
        return acc

    scores(0)
    for i in range(len(q_list)):
        if i + 1 < len(q_list):
            scores(i + 1)
        emit(i, finish(i))


def _attn_a_kernel(q_ref, k_ref, v_ref, o_ref, s_ref, *, tq):
    lo = lax.broadcasted_iota(jnp.int32, (STREAM_ROWS, LANES), 1) < HEAD_DIM
    streams = []
    for r in range(tq // STREAM_ROWS):
        for h in range(4):
            qp = q_ref[0, r * STREAM_ROWS:(r + 1) * STREAM_ROWS, (h // 2) * LANES:(h // 2 + 1) * LANES]
            keep = lo if h % 2 == 0 else jnp.logical_not(lo)
            streams.append(jnp.where(keep, qp.astype(F32), 0.0).astype(BF16))
    even = {}

    def emit(i, raw):
        o = raw * pltpu.roll(1.0 / raw, HEAD_DIM, 1)
        r, h = divmod(i, 4)
        if h % 2 == 0:
            even[i] = o
        else:
            o_ref[0, r * STREAM_ROWS:(r + 1) * STREAM_ROWS, (h // 2) * LANES:(h // 2 + 1) * LANES] = (
                jnp.where(lo, even.pop(i - 1), o).astype(BF16))

    _attend_streams(streams, k_ref, v_ref, [((i % 2) * LANES, LANES) for i in range(len(streams))], s_ref, emit)


def _attn_a(q, kd, vd):
    b, seq, d = q.shape
    tq = STREAMS_PER_STEP // 4 * STREAM_ROWS
    return pl.pallas_call(
        functools.partial(_attn_a_kernel, tq=tq),
        grid=(b, A_KV_HEADS, seq // tq),
        in_specs=[pl.BlockSpec((1, tq, MXU_DIM), lambda bi, g, i: (bi, i, g)),
                  pl.BlockSpec((1, seq, LANES), lambda bi, g, i: (bi, 0, g)),
                  pl.BlockSpec((1, seq, MXU_DIM), lambda bi, g, i: (bi, 0, g))],
        out_specs=pl.BlockSpec((1, tq, MXU_DIM), lambda bi, g, i: (bi, i, g)),
        out_shape=jax.ShapeDtypeStruct((b, seq, d), BF16),
        scratch_shapes=[pltpu.VMEM((2, STREAM_ROWS, seq), F32)],
        compiler_params=_params(3),
        name="attn_gqa",
    )(q, kd, vd)


def _attn_c_kernel(lqk_ref, subln_ref, q_ref, k_ref, v_ref, o_ref, s_ref, *, tq, lambda_init):
    lqk = lqk_ref[...]
    lam = (jnp.exp(jnp.sum(lqk[0:1] * lqk[1:2], axis=-1, keepdims=True))
           - jnp.exp(jnp.sum(lqk[2:3] * lqk[3:4], axis=-1, keepdims=True)) + lambda_init)
    lo = lax.broadcasted_iota(jnp.int32, (STREAM_ROWS, LANES), 1) < HEAD_DIM
    streams = []
    for r in range(tq // STREAM_ROWS):
        qp = q_ref[0, r * STREAM_ROWS:(r + 1) * STREAM_ROWS, :].astype(F32)
        streams.append(jnp.where(lo, qp, 0.0).astype(BF16))
        streams.append(jnp.where(lo, 0.0, qp).astype(BF16))
    first = {}

    def emit(i, raw):
        a = raw[:, :LANES] / raw[:, LANES:]
        r, comp = divmod(i, 2)
        if comp == 0:
            first[i] = a
            return
        o = first.pop(i - 1) - lam * a
        ms = jnp.mean(o * o, axis=-1, keepdims=True)
        o = o * lax.rsqrt(ms + RMS_EPS) * subln_ref[...] * (1.0 - lambda_init)
        o_ref[0, r * STREAM_ROWS:(r + 1) * STREAM_ROWS, :] = o.astype(BF16)

    _attend_streams(streams, k_ref, v_ref, [(0, 2 * LANES)] * len(streams), s_ref, emit)


def _attn_c(q, k, v, lqk, subln, lambda_init):
    b, seq, d = q.shape
    tq = STREAMS_PER_STEP // 2 * STREAM_ROWS
    blk = lambda bi, h, i: (bi, i, h)
    kv = lambda bi, h, i: (bi, 0, h)
    return pl.pallas_call(
        functools.partial(_attn_c_kernel, tq=tq, lambda_init=lambda_init),
        grid=(b, C_HEADS, seq // tq),
        in_specs=[_const_spec((4, HEAD_DIM)), _const_spec((1, LANES)),
                  pl.BlockSpec((1, tq, LANES), blk),
                  pl.BlockSpec((1, seq, LANES), kv),
                  pl.BlockSpec((1, seq, 2 * LANES), kv)],
        out_specs=pl.BlockSpec((1, tq, LANES), blk),
        out_shape=jax.ShapeDtypeStruct((b, seq, d), BF16),
        scratch_shapes=[pltpu.VMEM((2, STREAM_ROWS, seq), F32)],
        compiler_params=_params(3),
        name="attn_diff",
    )(lqk, subln[None, :], q, k, v)


def _qkv_c_kernel(x_ref, w_ref, c_ref, sa_ref, sb_ref, q_ref, k_ref, v_ref):
    xb = x_ref[...].astype(BF16)
    c, sa, sb = c_ref[...], sa_ref[...], sb_ref[...]
    d = x_ref.shape[1]
    n_chunks = d // MXU_DIM
    for j in range(3 * n_chunks):
        y = jnp.dot(xb, w_ref[:, j * MXU_DIM:(j + 1) * MXU_DIM], preferred_element_type=F32)
        which, jj = divmod(j, n_chunks)
        if which == 2:
            for t in range(2):
                col = (2 * jj + t) * MXU_DIM
                v_ref[:, col:col + LANES] = y[:, t * LANES:(t + 1) * LANES].astype(BF16)
                v_ref[:, col + LANES:col + 2 * LANES] = jnp.ones((y.shape[0], LANES), BF16)
            continue
        for t in range(2):
            r = _rope(y[:, t * LANES:(t + 1) * LANES], c, sa, sb, PARTIAL_ROPE_DIM // 2)
            col = jj * MXU_DIM + t * LANES
            if which == 0:
                q_ref[:, col:col + LANES] = (r * ATTN_SCALE).astype(BF16)
            else:
                k_ref[:, col:col + LANES] = r.astype(BF16)


def _qkv_c(x2, w, tabs, seq):
    n, d = x2.shape
    tm = ROW_TILE
    c, sa, sb = tabs
    tab_spec = pl.BlockSpec((tm, LANES), lambda i: (i % (seq // tm), 0))
    row = pl.BlockSpec((tm, d), lambda i: (i, 0))
    return pl.pallas_call(
        _qkv_c_kernel,
        grid=(n // tm,),
        in_specs=[row, _const_spec((d, 3 * d)), tab_spec, tab_spec, tab_spec],
        out_specs=[row, row, pl.BlockSpec((tm, 2 * d), lambda i: (i, 0))],
        out_shape=[jax.ShapeDtypeStruct((n, d), BF16), jax.ShapeDtypeStruct((n, d), BF16),
                   jax.ShapeDtypeStruct((n, 2 * d), BF16)],
        compiler_params=_params(1),
        name="qkv_diff",
    )(x2, w.astype(BF16), c, sa, sb)


def _proj_ln_kernel(a_ref, w_ref, bias_ref, x_ref, g_ref, b_ref, o_ref):
    sub = jnp.dot(a_ref[...], w_ref[...], preferred_element_type=F32) + bias_ref[...]
    o_ref[...] = _layernorm(DEEPNORM_ALPHA * x_ref[...] + sub, g_ref[...], b_ref[...])


def _proj_ln(a2, w, bias, x2, g, b):
    n, d = x2.shape
    tm = ROW_TILE
    row = pl.BlockSpec((tm, d), lambda i: (i, 0))
    vec = _const_spec((1, d))
    return pl.pallas_call(
        _proj_ln_kernel,
        grid=(n // tm,),
        in_specs=[row, _const_spec((d, d)), vec, row, vec, vec],
        out_specs=row,
        out_shape=jax.ShapeDtypeStruct((n, d), F32),
        compiler_params=_params(1),
        name="proj_ln",
    )(a2, w.astype(BF16), bias[None, :], x2, g[None, :], b[None, :])


def _ffn_kernel(x_ref, xp_ref, xn_ref, wup_ref, cw_ref, cb_ref, wdn_ref, g_ref, b_ref, o_ref,
                xe_ref, h_ref, a_ref, *, tiles_per_seq):
    tm = x_ref.shape[0]
    t = pl.program_id(0) % tiles_per_seq
    x = x_ref[...]
    xe_ref[0:HALO, :] = jnp.where(t == 0, 0.0, xp_ref[...]).astype(BF16)
    xe_ref[HALO:HALO + tm, :] = x.astype(BF16)
    xe_ref[HALO + tm:2 * HALO + tm, :] = jnp.where(t == tiles_per_seq - 1, 0.0, xn_ref[...]).astype(BF16)
    n_chunks = wup_ref.shape[0]
    for ci in range(n_chunks):
        h_ref[...] = jnp.dot(xe_ref[...], wup_ref[ci], preferred_element_type=F32)
        cw = cw_ref[ci]
        hc = (cb_ref[ci]
              + h_ref[HALO - 1:HALO - 1 + tm, :] * cw[0:1]
              + h_ref[HALO:HALO + tm, :] * cw[1:2]
              + h_ref[HALO + 1:HALO + 1 + tm, :] * cw[2:3])
        gate = hc[:, :FFN_CHUNK]
        val = hc[:, FFN_CHUNK:]
        act = gate * (1.0 / (1.0 + jnp.exp(-gate))) * val
        a_ref[:, ci * FFN_CHUNK:(ci + 1) * FFN_CHUNK] = act.astype(BF16)
    sub = jnp.dot(a_ref[...], wdn_ref[...], preferred_element_type=F32)
    o_ref[...] = _layernorm(DEEPNORM_ALPHA * x + sub, g_ref[...], b_ref[...])


def _ffn(x2, w_up, conv_w, conv_b, w_down, g, b, seq):
    n, d = x2.shape
    tm = ROW_TILE
    d_ff = w_down.shape[0]
    n_chunks = d_ff // FFN_CHUNK
    def chunked(m):
        lead = m.shape[0]
        gv = m.reshape(lead, 2, n_chunks, FFN_CHUNK)
        return jnp.transpose(gv, (2, 0, 1, 3)).reshape(n_chunks, lead, 2 * FFN_CHUNK)
    wup_c = chunked(w_up.astype(BF16))
    cw_c = chunked(conv_w)
    cb_c = chunked(conv_b[None, :])
    row = pl.BlockSpec((tm, d), lambda i: (i, 0))
    hb = tm // HALO
    prev = pl.BlockSpec((HALO, d), lambda i: (jnp.maximum(i * hb - 1, 0), 0))
    nxt = pl.BlockSpec((HALO, d), lambda i: (jnp.minimum((i + 1) * hb, n // HALO - 1), 0))
    vec = _const_spec((1, d))
    return pl.pallas_call(
        functools.partial(_ffn_kernel, tiles_per_seq=seq // tm),
        grid=(n // tm,),
        in_specs=[row, prev, nxt,
                  _const_spec(wup_c.shape), _const_spec(cw_c.shape), _const_spec(cb_c.shape),
                  _const_spec((d_ff, d)), vec, vec],
        out_specs=row,
        out_shape=jax.ShapeDtypeStruct((n, d), F32),
        scratch_shapes=[pltpu.VMEM((tm + 2 * HALO, d), BF16),
                        pltpu.VMEM((tm + 2 * HALO, 2 * FFN_CHUNK), F32),
                        pltpu.VMEM((tm, d_ff), BF16)],
        compiler_params=_params(1),
        name="conv_ffn",
    )(x2, x2, x2, wup_c, cw_c, cb_c, w_down.astype(BF16), g[None, :], b[None, :])


def _dft_chan_kernel(x_ref, cs_ref, y_ref):
    gw = cs_ref.shape[0]
    for g in range(x_ref.shape[2] // gw):
        xg = x_ref[0, :, g * gw:(g + 1) * gw].astype(BF16)
        y = jnp.dot(xg, cs_ref[...], preferred_element_type=F32)
        y_ref[0, 0, :, g * gw:(g + 1) * gw] = y[:, :gw].astype(BF16)
        y_ref[0, 1, :, g * gw:(g + 1) * gw] = y[:, gw:].astype(BF16)


def _dft_seq_kernel(m_ref, y_ref, o_ref):
    o_ref[0] = jnp.dot(m_ref[...], y_ref[0], preferred_element_type=F32).astype(BF16)


def _dft_matrices(n):
    idx = jnp.arange(n, dtype=jnp.int32)
    ang = ((idx[:, None] * idx[None, :]) % n).astype(F32) * (2.0 * math.pi / n)
    scale = n ** -0.5
    return jnp.cos(ang) * scale, jnp.sin(ang) * scale


def _fourier(x):
    b, seq, d = x.shape
    gw = d // FNET_GROUPS
    tm = ROW_TILE
    cc, sc = _dft_matrices(gw)
    cs_chan = jnp.concatenate([cc, sc], axis=1).astype(BF16)
    y = pl.pallas_call(
        _dft_chan_kernel,
        grid=(b, seq // tm),
        in_specs=[pl.BlockSpec((1, tm, d), lambda bi, i: (bi, i, 0)), _const_spec((gw, 2 * gw))],
        out_specs=pl.BlockSpec((1, 2, tm, d), lambda bi, i: (bi, 0, i, 0)),
        out_shape=jax.ShapeDtypeStruct((b, 2, seq, d), BF16),
        compiler_params=_params(2),
        name="dft_channels",
    )(x, cs_chan)
    cs, ss = _dft_matrices(seq)
    m_seq = jnp.concatenate([cs, -ss], axis=1).astype(BF16)
    tn = d // 2
    return pl.pallas_call(
        _dft_seq_kernel,
        grid=(seq // tm, b, d // tn),
        in_specs=[pl.BlockSpec((tm, 2 * seq), lambda i, bi, j: (i, 0)),
                  pl.BlockSpec((1, 2 * seq, tn), lambda i, bi, j: (bi, 0, j))],
        out_specs=pl.BlockSpec((1, tm, tn), lambda i, bi, j: (bi, i, j)),
        out_shape=jax.ShapeDtypeStruct((b, seq, d), BF16),
        compiler_params=_params(3),
        name="dft_sequence",
    )(m_seq, y.reshape(b, 2 * seq, d))


def _cos_sin(pos, dim, theta):
    inv_freq = theta ** (-jnp.arange(0, dim, 2, dtype=F32) / dim)
    ang = pos.astype(F32)[:, None] * inv_freq[None, :]
    return jnp.cos(ang), jnp.sin(ang)


def _axial_tables(seq):
    rows = seq // GRID_W
    t_row = jnp.repeat(jnp.arange(rows, dtype=jnp.int32), GRID_W)
    t_col = jnp.tile(jnp.arange(GRID_W, dtype=jnp.int32), rows)
    cr, sr = _cos_sin(t_row, HEAD_DIM // 2, AXIAL_THETA)
    cc, sc = _cos_sin(t_col, HEAD_DIM // 2, AXIAL_THETA)
    z = jnp.zeros_like(sr)
    c = jnp.concatenate([cr, cr, cc, cc], axis=1)
    sa = jnp.concatenate([-sr, z, -sc, z], axis=1)
    sb = jnp.concatenate([z, sr, z, sc], axis=1)
    return tuple(jnp.tile(t, (1, LANES // HEAD_DIM)) for t in (c, sa, sb))


def _partial_tables(seq):
    co, si = _cos_sin(jnp.arange(seq, dtype=jnp.int32), PARTIAL_ROPE_DIM, ROPE_THETA)
    rest = HEAD_DIM - PARTIAL_ROPE_DIM
    z = jnp.zeros_like(si)
    zr = jnp.zeros((seq, rest), F32)
    c = jnp.concatenate([co, co, jnp.ones((seq, rest), F32)], axis=1)
    sa = jnp.concatenate([-si, z, zr], axis=1)
    sb = jnp.concatenate([z, si, zr], axis=1)
    return tuple(jnp.tile(t, (1, LANES // HEAD_DIM)) for t in (c, sa, sb))


def _lambda_init(layer_idx):
    return 0.8 - 0.6 * math.exp(-0.3 * layer_idx)


def kernel(x, l0_a_wqkv, l0_a_qnorm, l0_a_knorm, l0_a_wo, l0_ln1_g, l0_ln1_b, l0_ffn_wup, l0_ffn_conv_w, l0_ffn_conv_b, l0_ffn_wdown, l0_ln2_g, l0_ln2_b, l1_f_wo, l1_f_bo, l1_ln1_g, l1_ln1_b, l1_ffn_wup, l1_ffn_conv_w, l1_ffn_conv_b, l1_ffn_wdown, l1_ln2_g, l1_ln2_b, l2_c_wqkv, l2_c_lq1, l2_c_lk1, l2_c_lq2, l2_c_lk2, l2_c_subln, l2_c_wo, l2_ln1_g, l2_ln1_b, l2_ffn_wup, l2_ffn_conv_w, l2_ffn_conv_b, l2_ffn_wdown, l2_ln2_g, l2_ln2_b, l3_a_wqkv, l3_a_qnorm, l3_a_knorm, l3_a_wo, l3_ln1_g, l3_ln1_b, l3_ffn_wup, l3_ffn_conv_w, l3_ffn_conv_b, l3_ffn_wdown, l3_ln2_g, l3_ln2_b):
    b, seq, d = x.shape
    n = b * seq
    assert seq % ROW_TILE == 0 and seq % (STREAMS_PER_STEP // 2 * STREAM_ROWS) == 0
    assert d == A_HEADS * HEAD_DIM
    axial = _axial_tables(seq)
    partial_t = _partial_tables(seq)
    zero_bias = jnp.zeros((d,), F32)

    def gqa_layer(x2, wqkv, qn, kn, wo, g, bb):
        q, kd, vd = _qkv_a(x2, wqkv, qn, kn, axial, seq)
        o = _attn_a(q.reshape(b, seq, d), kd.reshape(b, seq, -1), vd.reshape(b, seq, -1))
        return _proj_ln(o.reshape(n, d), wo, zero_bias, x2, g, bb)

    x2 = x.reshape(n, d)
    x2 = gqa_layer(x2, l0_a_wqkv, l0_a_qnorm, l0_a_knorm, l0_a_wo, l0_ln1_g, l0_ln1_b)
    x2 = _ffn(x2, l0_ffn_wup, l0_ffn_conv_w, l0_ffn_conv_b, l0_ffn_wdown, l0_ln2_g, l0_ln2_b, seq)
    f = _fourier(x2.reshape(b, seq, d))
    x2 = _proj_ln(f.reshape(n, d), l1_f_wo, l1_f_bo, x2, l1_ln1_g, l1_ln1_b)
    x2 = _ffn(x2, l1_ffn_wup, l1_ffn_conv_w, l1_ffn_conv_b, l1_ffn_wdown, l1_ln2_g, l1_ln2_b, seq)
    q, k, v = _qkv_c(x2, l2_c_wqkv, partial_t, seq)
    lqk = jnp.stack([l2_c_lq1, l2_c_lk1, l2_c_lq2, l2_c_lk2]).astype(F32)
    o = _attn_c(q.reshape(b, seq, d), k.reshape(b, seq, d), v.reshape(b, seq, 2 * d), lqk, l2_c_subln,
                _lambda_init(2))
    x2 = _proj_ln(o.reshape(n, d), l2_c_wo, zero_bias, x2, l2_ln1_g, l2_ln1_b)
    x2 = _ffn(x2, l2_ffn_wup, l2_ffn_conv_w, l2_ffn_conv_b, l2_ffn_wdown, l2_ln2_g, l2_ln2_b, seq)
    x2 = gqa_layer(x2, l3_a_wqkv, l3_a_qnorm, l3_a_knorm, l3_a_wo, l3_ln1_g, l3_ln1_b)
    x2 = _ffn(x2, l3_ffn_wup, l3_ffn_conv_w, l3_ffn_conv_b, l3_ffn_wdown, l3_ln2_g, l3_ln2_b, seq)
    return x2.reshape(b, seq, d)
```

```python
import functools
import math

import jax
import jax.numpy as jnp
from jax import lax
from jax.experimental import pallas as pl
from jax.experimental.pallas import tpu as pltpu

F32 = jnp.float32
BF16 = jnp.bfloat16

DEPTH = 4
GRID_W = 64
A_HEADS = 16
A_KV_HEADS = 4
HEAD_DIM = 64
AXIAL_THETA = 10000.0
FNET_GROUPS = 4
C_HEADS = 8
PARTIAL_ROPE_DIM = 16
ROPE_THETA = 500000.0
CONV_WIDTH = 3
DEEPNORM_ALPHA = (2.0 * DEPTH) ** 0.25
LN_EPS = 1e-5
RMS_EPS = 1e-6
ATTN_SCALE = HEAD_DIM ** -0.5 * math.log2(math.e)

LANES = 128
MXU_DIM = 256
BF16_SUBLANES = 16
VMEM_LIMIT = 56 * 1024 * 1024

ROW_TILE = 512
FFN_CHUNK = 256
HALO = BF16_SUBLANES
CONV_ROWS = 64
STREAM_ROWS = 256
KEY_TILE = MXU_DIM
STREAMS_PER_STEP = 8


def _const_spec(shape):
    return pl.BlockSpec(shape, lambda *_: (0,) * len(shape), pipeline_mode=pl.Buffered(1))


def _params(n_axes):
    return pltpu.CompilerParams(dimension_semantics=("arbitrary",) * n_axes,
                                vmem_limit_bytes=VMEM_LIMIT)


def _layernorm(y, g, b):
    mu = jnp.mean(y, axis=-1, keepdims=True)
    d = y - mu
    var = jnp.mean(d * d, axis=-1, keepdims=True)
    return d * lax.rsqrt(var + LN_EPS) * g + b


def _rope(y, c, sa, sb, d):
    return y * c + pltpu.roll(y, LANES - d, 1) * sa + pltpu.roll(y, d, 1) * sb


def _dup_halves(blk):
    swapped = pltpu.roll(blk, HEAD_DIM, 1)
    lo = lax.broadcasted_iota(jnp.int32, blk.shape, 1) < HEAD_DIM
    return jnp.where(lo, blk, swapped), jnp.where(lo, swapped, blk)


def _qkv_a_kernel(x_ref, w_ref, bd_ref, gq_ref, gk_ref, c_ref, sa_ref, sb_ref, q_ref, k_ref, v_ref):
    xb = x_ref[...].astype(BF16)
    c, sa, sb = c_ref[...], sa_ref[...], sb_ref[...]
    bd = bd_ref[...]
    n_q = A_HEADS * HEAD_DIM // MXU_DIM

    def head_norm(y, g):
        sq = y * y
        hi = sq.astype(BF16)
        lo = (sq - hi.astype(F32)).astype(BF16)
        ms = (jnp.dot(hi, bd, preferred_element_type=F32) + jnp.dot(lo, bd, preferred_element_type=F32))
        return y * lax.rsqrt(ms + RMS_EPS) * g

    def proj(j):
        return jnp.dot(xb, w_ref[:, j * MXU_DIM:(j + 1) * MXU_DIM], preferred_element_type=F32)

    y_next = proj(0)
    for j in range(n_q + 2):
        y = y_next
        if j + 1 < n_q + 2:
            y_next = proj(j + 1)
        if j < n_q:
            yn = head_norm(y, gq_ref[...])
            for t in range(2):
                r = _rope(yn[:, t * LANES:(t + 1) * LANES], c, sa, sb, HEAD_DIM // 4)
                q_ref[:, j * MXU_DIM + t * LANES:j * MXU_DIM + (t + 1) * LANES] = (r * ATTN_SCALE).astype(BF16)
        elif j == n_q:
            yn = head_norm(y, gk_ref[...])
            for t in range(2):
                r = _rope(yn[:, t * LANES:(t + 1) * LANES], c, sa, sb, HEAD_DIM // 4)
                d0, d1 = _dup_halves(r)
                k_ref[:, (2 * t) * LANES:(2 * t + 1) * LANES] = d0.astype(BF16)
                k_ref[:, (2 * t + 1) * LANES:(2 * t + 2) * LANES] = d1.astype(BF16)
        else:
            lo_half = lax.broadcasted_iota(jnp.int32, (y.shape[0], LANES), 1) < HEAD_DIM
            for t in range(2):
                for u, dup in enumerate(_dup_halves(y[:, t * LANES:(t + 1) * LANES])):
                    col = (2 * t + u) * MXU_DIM
                    v_ref[:, col:col + LANES] = jnp.where(lo_half, dup, 1.0).astype(BF16)
                    v_ref[:, col + LANES:col + 2 * LANES] = jnp.where(lo_half, 1.0, dup).astype(BF16)


def _qkv_a(x2, w, gq, gk, tabs, seq):
    n, d = x2.shape
    tm = ROW_TILE
    n_out = w.shape[1]
    c, sa, sb = tabs
    bd = (jnp.arange(MXU_DIM)[:, None] // HEAD_DIM == jnp.arange(MXU_DIM)[None, :] // HEAD_DIM)
    bd = (bd.astype(F32) / HEAD_DIM).astype(BF16)
    gq2 = jnp.tile(gq, MXU_DIM // HEAD_DIM)[None, :]
    gk2 = jnp.tile(gk, MXU_DIM // HEAD_DIM)[None, :]
    tab_spec = pl.BlockSpec((tm, LANES), lambda i: (i % (seq // tm), 0))
    kvw = A_KV_HEADS * LANES
    return pl.pallas_call(
        _qkv_a_kernel,
        grid=(n // tm,),
        in_specs=[pl.BlockSpec((tm, d), lambda i: (i, 0)),
                  _const_spec((d, n_out)), _const_spec((MXU_DIM, MXU_DIM)),
                  _const_spec((1, MXU_DIM)), _const_spec((1, MXU_DIM)),
                  tab_spec, tab_spec, tab_spec],
        out_specs=[pl.BlockSpec((tm, d), lambda i: (i, 0)),
                   pl.BlockSpec((tm, kvw), lambda i: (i, 0)),
                   pl.BlockSpec((tm, 2 * kvw), lambda i: (i, 0))],
        out_shape=[jax.ShapeDtypeStruct((n, d), BF16),
                   jax.ShapeDtypeStruct((n, kvw), BF16),
                   jax.ShapeDtypeStruct((n, 2 * kvw), BF16)],
        compiler_params=_params(1),
        name="qkv_gqa",
    )(x2, w.astype(BF16), bd, gq2, gk2, c, sa, sb)


def _attend_streams(q_list, k_ref, v_ref, v_cols, s_ref, emit):
    rows, seq = s_ref.shape[1], s_ref.shape[2]
    k = k_ref[0]

    def scores(i):
        s_ref[i % 2] = lax.dot_general(q_list[i], k, (((1,), (1,)), ((), ())), preferred_element_type=F32)

    def finish(i):
        sb = s_ref.at[i % 2]
        c0, cw = v_cols[i]
        m = jnp.broadcast_to(jnp.max(sb[...], axis=-1, keepdims=True), (rows, KEY_TILE))
        acc = None
        for kt in range(seq // KEY_TILE):
            p = jnp.exp2(sb[:, kt * KEY_TILE:(kt + 1) * KEY_TILE] - m)
            part = jnp.dot(p.astype(BF16), v_ref[0, kt * KEY_TILE:(kt + 1) * KEY_TILE, c0:c0 + cw],
                           preferred_element_type=F32)
            acc = part if acc is None else acc + part
        return acc

    scores(0)
    for i in range(len(q_list)):
        if i + 1 < len(q_list):
            scores(i + 1)
        emit(i, finish(i))


def _attn_a_kernel(q_ref, k_ref, v_ref, o_ref, s_ref, *, tq):
    lo = lax.broadcasted_iota(jnp.int32, (STREAM_ROWS, LANES), 1) < HEAD_DIM
    streams = []
    for r in range(tq // STREAM_ROWS):
        for h in range(4):
            qp = q_ref[0, r * STREAM_ROWS:(r + 1) * STREAM_ROWS, (h // 2) * LANES:(h // 2 + 1) * LANES]
            keep = lo if h % 2 == 0 else jnp.logical_not(lo)
            streams.append(jnp.where(keep, qp.astype(F32), 0.0).astype(BF16))
    even = {}

    def emit(i, raw):
        o = raw * pltpu.roll(1.0 / raw, HEAD_DIM, 1)
        r, h = divmod(i, 4)
        if h % 2 == 0:
            even[i] = o
        else:
            o_ref[0, r * STREAM_ROWS:(r + 1) * STREAM_ROWS, (h // 2) * LANES:(h // 2 + 1) * LANES] = (
                jnp.where(lo, even.pop(i - 1), o).astype(BF16))

    _attend_streams(streams, k_ref, v_ref, [((i % 2) * LANES, LANES) for i in range(len(streams))], s_ref, emit)


def _attn_a(q, kd, vd):
    b, seq, d = q.shape
    tq = STREAMS_PER_STEP // 4 * STREAM_ROWS
    return pl.pallas_call(
        functools.partial(_attn_a_kernel, tq=tq),
        grid=(b, A_KV_HEADS, seq // tq),
        in_specs=[pl.BlockSpec((1, tq, MXU_DIM), lambda bi, g, i: (bi, i, g)),
                  pl.BlockSpec((1, seq, LANES), lambda bi, g, i: (bi, 0, g)),
                  pl.BlockSpec((1, seq, MXU_DIM), lambda bi, g, i: (bi, 0, g))],
        out_specs=pl.BlockSpec((1, tq, MXU_DIM), lambda bi, g, i: (bi, i, g)),
        out_shape=jax.ShapeDtypeStruct((b, seq, d), BF16),
        scratch_shapes=[pltpu.VMEM((2, STREAM_ROWS, seq), F32)],
        compiler_params=_params(3),
        name="attn_gqa",
    )(q, kd, vd)


def _attn_c_kernel(lqk_ref, subln_ref, q_ref, k_ref, v_ref, o_ref, s_ref, *, tq, lambda_init):
    lqk = lqk_ref[...]
    lam = (jnp.exp(jnp.sum(lqk[0:1] * lqk[1:2], axis=-1, keepdims=True))
           - jnp.exp(jnp.sum(lqk[2:3] * lqk[3:4], axis=-1, keepdims=True)) + lambda_init)
    lo = lax.broadcasted_iota(jnp.int32, (STREAM_ROWS, LANES), 1) < HEAD_DIM
    streams = []
    for r in range(tq // STREAM_ROWS):
        qp = q_ref[0, r * STREAM_ROWS:(r + 1) * STREAM_ROWS, :].astype(F32)
        streams.append(jnp.where(lo, qp, 0.0).astype(BF16))
        streams.append(jnp.where(lo, 0.0, qp).astype(BF16))
    first = {}

    def emit(i, raw):
        a = raw[:, :LANES] / raw[:, LANES:]
        r, comp = divmod(i, 2)
        if comp == 0:
            first[i] = a
            return
        o = first.pop(i - 1) - lam * a
        ms = jnp.mean(o * o, axis=-1, keepdims=True)
        o = o * lax.rsqrt(ms + RMS_EPS) * subln_ref[...] * (1.0 - lambda_init)
        o_ref[0, r * STREAM_ROWS:(r + 1) * STREAM_ROWS, :] = o.astype(BF16)

    _attend_streams(streams, k_ref, v_ref, [(0, 2 * LANES)] * len(streams), s_ref, emit)


def _attn_c(q, k, v, lqk, subln, lambda_init):
    b, seq, d = q.shape
    tq = STREAMS_PER_STEP // 2 * STREAM_ROWS
    blk = lambda bi, h, i: (bi, i, h)
    kv = lambda bi, h, i: (bi, 0, h)
    return pl.pallas_call(
        functools.partial(_attn_c_kernel, tq=tq, lambda_init=lambda_init),
        grid=(b, C_HEADS, seq // tq),
        in_specs=[_const_spec((4, HEAD_DIM)), _const_spec((1, LANES)),
                  pl.BlockSpec((1, tq, LANES), blk),
                  pl.BlockSpec((1, seq, LANES), kv),
                  pl.BlockSpec((1, seq, 2 * LANES), kv)],
        out_specs=pl.BlockSpec((1, tq, LANES), blk),
        out_shape=jax.ShapeDtypeStruct((b, seq, d), BF16),
        scratch_shapes=[pltpu.VMEM((2, STREAM_ROWS, seq), F32)],
        compiler_params=_params(3),
        name="attn_diff",
    )(lqk, subln[None, :], q, k, v)


def _qkv_c_kernel(x_ref, w_ref, c_ref, sa_ref, sb_ref, q_ref, k_ref, v_ref):
    xb = x_ref[...].astype(BF16)
    c, sa, sb = c_ref[...], sa_ref[...], sb_ref[...]
    d = x_ref.shape[1]
    n_chunks = d // MXU_DIM
    for j in range(3 * n_chunks):
        y = jnp.dot(xb, w_ref[:, j * MXU_DIM:(j + 1) * MXU_DIM], preferred_element_type=F32)
        which, jj = divmod(j, n_chunks)
        if which == 2:
            for t in range(2):
                col = (2 * jj + t) * MXU_DIM
                v_ref[:, col:col + LANES] = y[:, t * LANES:(t + 1) * LANES].astype(BF16)
                v_ref[:, col + LANES:col + 2 * LANES] = jnp.ones((y.shape[0], LANES), BF16)
            continue
        for t in range(2):
            r = _rope(y[:, t * LANES:(t + 1) * LANES], c, sa, sb, PARTIAL_ROPE_DIM // 2)
            col = jj * MXU_DIM + t * LANES
            if which == 0:
                q_ref[:, col:col + LANES] = (r * ATTN_SCALE).astype(BF16)
            else:
                k_ref[:, col:col + LANES] = r.astype(BF16)


def _qkv_c(x2, w, tabs, seq):
    n, d = x2.shape
    tm = ROW_TILE
    c, sa, sb = tabs
    tab_spec = pl.BlockSpec((tm, LANES), lambda i: (i % (seq // tm), 0))
    row = pl.BlockSpec((tm, d), lambda i: (i, 0))
    return pl.pallas_call(
        _qkv_c_kernel,
        grid=(n // tm,),
        in_specs=[row, _const_spec((d, 3 * d)), tab_spec, tab_spec, tab_spec],
        out_specs=[row, row, pl.BlockSpec((tm, 2 * d), lambda i: (i, 0))],
        out_shape=[jax.ShapeDtypeStruct((n, d), BF16), jax.ShapeDtypeStruct((n, d), BF16),
                   jax.ShapeDtypeStruct((n, 2 * d), BF16)],
        compiler_params=_params(1),
        name="qkv_diff",
    )(x2, w.astype(BF16), c, sa, sb)


def _proj_ln_kernel(a_ref, w_ref, bias_ref, x_ref, g_ref, b_ref, o_ref):
    sub = jnp.dot(a_ref[...], w_ref[...], preferred_element_type=F32) + bias_ref[...]
    o_ref[...] = _layernorm(DEEPNORM_ALPHA * x_ref[...] + sub, g_ref[...], b_ref[...])


def _proj_ln(a2, w, bias, x2, g, b):
    n, d = x2.shape
    tm = ROW_TILE
    row = pl.BlockSpec((tm, d), lambda i: (i, 0))
    vec = _const_spec((1, d))
    return pl.pallas_call(
        _proj_ln_kernel,
        grid=(n // tm,),
        in_specs=[row, _const_spec((d, d)), vec, row, vec, vec],
        out_specs=row,
        out_shape=jax.ShapeDtypeStruct((n, d), F32),
        compiler_params=_params(1),
        name="proj_ln",
    )(a2, w.astype(BF16), bias[None, :], x2, g[None, :], b[None, :])


def _ffn_kernel(x_ref, xp_ref, xn_ref, wup_ref, cw_ref, cb_ref, wdn_ref, g_ref, b_ref, o_ref,
                xe_ref, h_ref, a_ref, *, tiles_per_seq):
    tm = x_ref.shape[0]
    t = pl.program_id(0) % tiles_per_seq
    xe_ref[0:HALO, :] = jnp.where(t == 0, 0.0, xp_ref[...]).astype(BF16)
    xe_ref[HALO:HALO + tm, :] = x_ref[...].astype(BF16)
    xe_ref[HALO + tm:2 * HALO + tm, :] = jnp.where(t == tiles_per_seq - 1, 0.0, xn_ref[...]).astype(BF16)
    n_chunks = wup_ref.shape[0]
    n_lt = FFN_CHUNK // LANES
    for ci in range(n_chunks):
        h = jnp.dot(xe_ref[...], wup_ref[ci], preferred_element_type=F32)
        for j in range(2 * n_lt):
            h_ref[j] = h[:, j * LANES:(j + 1) * LANES]
        cw = cw_ref[ci]
        cb = cb_ref[ci]

        def conv(j, r0):
            lanes = slice(j * LANES, (j + 1) * LANES)
            return (cb[:, lanes]
                    + h_ref[j, HALO - 1 + r0:HALO - 1 + r0 + CONV_ROWS, :] * cw[0:1, lanes]
                    + h_ref[j, HALO + r0:HALO + r0 + CONV_ROWS, :] * cw[1:2, lanes]
                    + h_ref[j, HALO + 1 + r0:HALO + 1 + r0 + CONV_ROWS, :] * cw[2:3, lanes])

        for r0 in range(0, tm, CONV_ROWS):
            for j in range(n_lt):
                gate = conv(j, r0)
                val = conv(n_lt + j, r0)
                act = gate * (1.0 / (1.0 + jnp.exp(-gate))) * val
                col = ci * FFN_CHUNK + j * LANES
                a_ref[r0:r0 + CONV_ROWS, col:col + LANES] = act.astype(BF16)
    sub = jnp.dot(a_ref[...], wdn_ref[...], preferred_element_type=F32)
    o_ref[...] = _layernorm(DEEPNORM_ALPHA * x_ref[...] + sub, g_ref[...], b_ref[...])


def _ffn(x2, w_up, conv_w, conv_b, w_down, g, b, seq):
    n, d = x2.shape
    tm = ROW_TILE
    d_ff = w_down.shape[0]
    n_chunks = d_ff // FFN_CHUNK
    def chunked(m):
        lead = m.shape[0]
        gv = m.reshape(lead, 2, n_chunks, FFN_CHUNK)
        return jnp.transpose(gv, (2, 0, 1, 3)).reshape(n_chunks, lead, 2 * FFN_CHUNK)
    wup_c = chunked(w_up.astype(BF16))
    cw_c = chunked(conv_w)
    cb_c = chunked(conv_b[None, :])
    row = pl.BlockSpec((tm, d), lambda i: (i, 0))
    hb = tm // HALO
    prev = pl.BlockSpec((HALO, d), lambda i: (jnp.maximum(i * hb - 1, 0), 0))
    nxt = pl.BlockSpec((HALO, d), lambda i: (jnp.minimum((i + 1) * hb, n // HALO - 1), 0))
    vec = _const_spec((1, d))
    return pl.pallas_call(
        functools.partial(_ffn_kernel, tiles_per_seq=seq // tm),
        grid=(n // tm,),
        in_specs=[row, prev, nxt,
                  _const_spec(wup_c.shape), _const_spec(cw_c.shape), _const_spec(cb_c.shape),
                  _const_spec((d_ff, d)), vec, vec],
        out_specs=row,
        out_shape=jax.ShapeDtypeStruct((n, d), F32),
        scratch_shapes=[pltpu.VMEM((tm + 2 * HALO, d), BF16),
                        pltpu.VMEM((2 * FFN_CHUNK // LANES, tm + 2 * HALO, LANES), F32),
                        pltpu.VMEM((tm, d_ff), BF16)],
        compiler_params=_params(1),
        name="conv_ffn",
    )(x2, x2, x2, wup_c, cw_c, cb_c, w_down.astype(BF16), g[None, :], b[None, :])


FFT_RADIX = 64
FFT_COLS = 8


def _dft_ab_kernel(x_ref, cs_ref, la_ref, tc_ref, ts_ref, bre_ref, bim_ref, y_ref):
    gw = cs_ref.shape[0]
    d = gw * FNET_GROUPS
    r = FFT_RADIX
    xs = jnp.concatenate([x_ref[0, :, t * d:(t + 1) * d] for t in range(FFT_COLS)], axis=0).astype(BF16)
    for g in range(FNET_GROUPS):
        yg = jnp.dot(xs[:, g * gw:(g + 1) * gw], cs_ref[...], preferred_element_type=F32)
        for t in range(FFT_COLS):
            y_ref[t, 0:r, g * gw:(g + 1) * gw] = yg[t * r:(t + 1) * r, :gw].astype(BF16)
            y_ref[t, r:2 * r, g * gw:(g + 1) * gw] = yg[t * r:(t + 1) * r, gw:].astype(BF16)
    for t in range(FFT_COLS):
        a = jnp.dot(la_ref[...], y_ref[t], preferred_element_type=F32)
        a_re, a_im = a[:r], a[r:]
        tc = jnp.tile(tc_ref[:, t * LANES:(t + 1) * LANES], (1, d // LANES))
        ts = jnp.tile(ts_ref[:, t * LANES:(t + 1) * LANES], (1, d // LANES))
        bre_ref[0, :, t * d:(t + 1) * d] = (a_re * tc + a_im * ts).astype(BF16)
        bim_ref[0, :, t * d:(t + 1) * d] = (a_im * tc - a_re * ts).astype(BF16)


def _dft_c_proj_ln_kernel(bre_ref, bim_ref, lc_ref, w_ref, bias_ref, x_ref, g_ref, b_ref, o_ref, f_ref):
    d = w_ref.shape[0]
    r = FFT_RADIX
    for t in range(FFT_COLS):
        rhs = jnp.concatenate([bre_ref[0, t * r:(t + 1) * r, :], bim_ref[0, t * r:(t + 1) * r, :]], axis=0)
        f_ref[t * r:(t + 1) * r, :] = jnp.dot(lc_ref[...], rhs, preferred_element_type=F32).astype(BF16)
    sub = jnp.dot(f_ref[...], w_ref[...], preferred_element_type=F32) + bias_ref[...]
    for t in range(FFT_COLS):
        y = DEEPNORM_ALPHA * x_ref[0, :, t * d:(t + 1) * d] + sub[t * r:(t + 1) * r]
        o_ref[0, :, t * d:(t + 1) * d] = _layernorm(y, g_ref[...], b_ref[...])


def _cos_sin_matrix(rows, cols, period):
    ang = ((rows[:, None] * cols[None, :]) % period).astype(F32) * (2.0 * math.pi / period)
    return jnp.cos(ang), jnp.sin(ang)


def _fourier_layer(x, w_o, b_o, g, bb):
    b, seq, d = x.shape
    r = FFT_RADIX
    assert seq == r * r and r % FFT_COLS == 0
    gw = d // FNET_GROUPS
    cidx = jnp.arange(gw, dtype=jnp.int32)
    cc, sc = _cos_sin_matrix(cidx, cidx, gw)
    cs_chan = (jnp.concatenate([cc, -sc], axis=1) * gw ** -0.5).astype(BF16)
    ridx = jnp.arange(r, dtype=jnp.int32)
    c64, s64 = (m * r ** -0.5 for m in _cos_sin_matrix(ridx, ridx, r))
    l_a = jnp.concatenate([jnp.concatenate([c64, s64], axis=1),
                           jnp.concatenate([-s64, c64], axis=1)], axis=0).astype(BF16)
    l_c = jnp.concatenate([c64, s64], axis=1).astype(BF16)
    tc, ts = _cos_sin_matrix(ridx, ridx, seq)
    tc = jnp.repeat(tc, LANES, axis=1)
    ts = jnp.repeat(ts, LANES, axis=1)
    wide = FFT_COLS * d
    xv = x.reshape(b, r, r * d)
    col_blk = pl.BlockSpec((1, r, wide), lambda bi, j: (bi, 0, j))
    tw_blk = pl.BlockSpec((r, FFT_COLS * LANES), lambda bi, j: (0, j))
    bre, bim = pl.pallas_call(
        _dft_ab_kernel,
        grid=(b, r // FFT_COLS),
        in_specs=[col_blk, _const_spec((gw, 2 * gw)), _const_spec((2 * r, 2 * r)), tw_blk, tw_blk],
        out_specs=[col_blk, col_blk],
        out_shape=[jax.ShapeDtypeStruct((b, r, r * d), BF16)] * 2,
        scratch_shapes=[pltpu.VMEM((FFT_COLS, 2 * r, d), BF16)],
        compiler_params=_params(2),
        name="dft_stage_ab",
    )(xv, cs_chan, l_a, tc, ts)
    row_blk = pl.BlockSpec((1, FFT_COLS * r, d), lambda bi, j: (bi, j, 0))
    vec = _const_spec((1, d))
    out = pl.pallas_call(
        _dft_c_proj_ln_kernel,
        grid=(b, r // FFT_COLS),
        in_specs=[row_blk, row_blk, _const_spec((r, 2 * r)), _const_spec((d, d)), vec, col_blk, vec, vec],
        out_specs=col_blk,
        out_shape=jax.ShapeDtypeStruct((b, r, r * d), F32),
        scratch_shapes=[pltpu.VMEM((FFT_COLS * r, d), BF16)],
        compiler_params=_params(2),
        name="dft_stage_c_proj_ln",
    )(bre.reshape(b, seq, d), bim.reshape(b, seq, d), l_c, w_o.astype(BF16), b_o[None, :], xv,
      g[None, :], bb[None, :])
    return out.reshape(b, seq, d)


def _cos_sin(pos, dim, theta):
    inv_freq = theta ** (-jnp.arange(0, dim, 2, dtype=F32) / dim)
    ang = pos.astype(F32)[:, None] * inv_freq[None, :]
    return jnp.cos(ang), jnp.sin(ang)


def _axial_tables(seq):
    rows = seq // GRID_W
    t_row = jnp.repeat(jnp.arange(rows, dtype=jnp.int32), GRID_W)
    t_col = jnp.tile(jnp.arange(GRID_W, dtype=jnp.int32), rows)
    cr, sr = _cos_sin(t_row, HEAD_DIM // 2, AXIAL_THETA)
    cc, sc = _cos_sin(t_col, HEAD_DIM // 2, AXIAL_THETA)
    z = jnp.zeros_like(sr)
    c = jnp.concatenate([cr, cr, cc, cc], axis=1)
    sa = jnp.concatenate([-sr, z, -sc, z], axis=1)
    sb = jnp.concatenate([z, sr, z, sc], axis=1)
    return tuple(jnp.tile(t, (1, LANES // HEAD_DIM)) for t in (c, sa, sb))


def _partial_tables(seq):
    co, si = _cos_sin(jnp.arange(seq, dtype=jnp.int32), PARTIAL_ROPE_DIM, ROPE_THETA)
    rest = HEAD_DIM - PARTIAL_ROPE_DIM
    z = jnp.zeros_like(si)
    zr = jnp.zeros((seq, rest), F32)
    c = jnp.concatenate([co, co, jnp.ones((seq, rest), F32)], axis=1)
    sa = jnp.concatenate([-si, z, zr], axis=1)
    sb = jnp.concatenate([z, si, zr], axis=1)
    return tuple(jnp.tile(t, (1, LANES // HEAD_DIM)) for t in (c, sa, sb))


def _lambda_init(layer_idx):
    return 0.8 - 0.6 * math.exp(-0.3 * layer_idx)


def kernel(x, l0_a_wqkv, l0_a_qnorm, l0_a_knorm, l0_a_wo, l0_ln1_g, l0_ln1_b, l0_ffn_wup, l0_ffn_conv_w, l0_ffn_conv_b, l0_ffn_wdown, l0_ln2_g, l0_ln2_b, l1_f_wo, l1_f_bo, l1_ln1_g, l1_ln1_b, l1_ffn_wup, l1_ffn_conv_w, l1_ffn_conv_b, l1_ffn_wdown, l1_ln2_g, l1_ln2_b, l2_c_wqkv, l2_c_lq1, l2_c_lk1, l2_c_lq2, l2_c_lk2, l2_c_subln, l2_c_wo, l2_ln1_g, l2_ln1_b, l2_ffn_wup, l2_ffn_conv_w, l2_ffn_conv_b, l2_ffn_wdown, l2_ln2_g, l2_ln2_b, l3_a_wqkv, l3_a_qnorm, l3_a_knorm, l3_a_wo, l3_ln1_g, l3_ln1_b, l3_ffn_wup, l3_ffn_conv_w, l3_ffn_conv_b, l3_ffn_wdown, l3_ln2_g, l3_ln2_b):
    b, seq, d = x.shape
    n = b * seq
    assert seq % ROW_TILE == 0 and seq % (STREAMS_PER_STEP // 2 * STREAM_ROWS) == 0
    assert d == A_HEADS * HEAD_DIM
    axial = _axial_tables(seq)
    partial_t = _partial_tables(seq)
    zero_bias = jnp.zeros((d,), F32)

    def gqa_layer(x2, wqkv, qn, kn, wo, g, bb):
        q, kd, vd = _qkv_a(x2, wqkv, qn, kn, axial, seq)
        o = _attn_a(q.reshape(b, seq, d), kd.reshape(b, seq, -1), vd.reshape(b, seq, -1))
        return _proj_ln(o.reshape(n, d), wo, zero_bias, x2, g, bb)

    x2 = x.reshape(n, d)
    x2 = gqa_layer(x2, l0_a_wqkv, l0_a_qnorm, l0_a_knorm, l0_a_wo, l0_ln1_g, l0_ln1_b)
    x2 = _ffn(x2, l0_ffn_wup, l0_ffn_conv_w, l0_ffn_conv_b, l0_ffn_wdown, l0_ln2_g, l0_ln2_b, seq)
    x2 = _fourier_layer(x2.reshape(b, seq, d), l1_f_wo, l1_f_bo, l1_ln1_g, l1_ln1_b).reshape(n, d)
    x2 = _ffn(x2, l1_ffn_wup, l1_ffn_conv_w, l1_ffn_conv_b, l1_ffn_wdown, l1_ln2_g, l1_ln2_b, seq)
    q, k, v = _qkv_c(x2, l2_c_wqkv, partial_t, seq)
    lqk = jnp.stack([l2_c_lq1, l2_c_lk1, l2_c_lq2, l2_c_lk2]).astype(F32)
    o = _attn_c(q.reshape(b, seq, d), k.reshape(b, seq, d), v.reshape(b, seq, 2 * d), lqk, l2_c_subln,
                _lambda_init(2))
    x2 = _proj_ln(o.reshape(n, d), l2_c_wo, zero_bias, x2, l2_ln1_g, l2_ln1_b)
    x2 = _ffn(x2, l2_ffn_wup, l2_ffn_conv_w, l2_ffn_conv_b, l2_ffn_wdown, l2_ln2_g, l2_ln2_b, seq)
    x2 = gqa_layer(x2, l3_a_wqkv, l3_a_qnorm, l3_a_knorm, l3_a_wo, l3_ln1_g, l3_ln1_b)
    x2 = _ffn(x2, l3_ffn_wup, l3_ffn_conv_w, l3_ffn_conv_b, l3_ffn_wdown, l3_ln2_g, l3_ln2_b, seq)
    return x2.reshape(b, seq, d)
```

```python
import functools
import math

import jax
import jax.numpy as jnp
from jax import lax
from jax.experimental import pallas as pl
from jax.experimental.pallas import tpu as pltpu

F32 = jnp.float32
BF16 = jnp.bfloat16

DEPTH = 4
GRID_W = 64
A_HEADS = 16
A_KV_HEADS = 4
HEAD_DIM = 64
AXIAL_THETA = 10000.0
FNET_GROUPS = 4
C_HEADS = 8
PARTIAL_ROPE_DIM = 16
ROPE_THETA = 500000.0
CONV_WIDTH = 3
DEEPNORM_ALPHA = (2.0 * DEPTH) ** 0.25
LN_EPS = 1e-5
RMS_EPS = 1e-6
ATTN_SCALE = HEAD_DIM ** -0.5 * math.log2(math.e)

LANES = 128
MXU_DIM = 256
BF16_SUBLANES = 16
VMEM_LIMIT = 56 * 1024 * 1024

ROW_TILE = 512
FFN_CHUNK = 256
HALO = BF16_SUBLANES
STREAM_ROWS = 256
KEY_TILE = MXU_DIM
STREAMS_PER_STEP = 8


def _const_spec(shape):
    return pl.BlockSpec(shape, lambda *_: (0,) * len(shape), pipeline_mode=pl.Buffered(1))


def _params(n_axes):
    return pltpu.CompilerParams(dimension_semantics=("arbitrary",) * n_axes,
                                vmem_limit_bytes=VMEM_LIMIT)


def _layernorm(y, g, b):
    mu = jnp.mean(y, axis=-1, keepdims=True)
    d = y - mu
    var = jnp.mean(d * d, axis=-1, keepdims=True)
    return d * lax.rsqrt(var + LN_EPS) * g + b


def _rope(y, c, sa, sb, d):
    return y * c + pltpu.roll(y, LANES - d, 1) * sa + pltpu.roll(y, d, 1) * sb


def _dup_halves(blk):
    swapped = pltpu.roll(blk, HEAD_DIM, 1)
    lo = lax.broadcasted_iota(jnp.int32, blk.shape, 1) < HEAD_DIM
    return jnp.where(lo, blk, swapped), jnp.where(lo, swapped, blk)


def _qkv_a_kernel(x_ref, w_ref, bd_ref, gq_ref, gk_ref, c_ref, sa_ref, sb_ref, q_ref, k_ref, v_ref):
    xb = x_ref[...].astype(BF16)
    c, sa, sb = c_ref[...], sa_ref[...], sb_ref[...]
    bd = bd_ref[...]
    n_q = A_HEADS * HEAD_DIM // MXU_DIM

    def head_norm(y, g):
        sq = y * y
        hi = sq.astype(BF16)
        lo = (sq - hi.astype(F32)).astype(BF16)
        ms = (jnp.dot(hi, bd, preferred_element_type=F32) + jnp.dot(lo, bd, preferred_element_type=F32))
        return y * lax.rsqrt(ms + RMS_EPS) * g

    def proj(j):
        return jnp.dot(xb, w_ref[:, j * MXU_DIM:(j + 1) * MXU_DIM], preferred_element_type=F32)

    y_next = proj(0)
    for j in range(n_q + 2):
        y = y_next
        if j + 1 < n_q + 2:
            y_next = proj(j + 1)
        if j < n_q:
            yn = head_norm(y, gq_ref[...])
            for t in range(2):
                r = _rope(yn[:, t * LANES:(t + 1) * LANES], c, sa, sb, HEAD_DIM // 4)
                q_ref[:, j * MXU_DIM + t * LANES:j * MXU_DIM + (t + 1) * LANES] = (r * ATTN_SCALE).astype(BF16)
        elif j == n_q:
            yn = head_norm(y, gk_ref[...])
            for t in range(2):
                r = _rope(yn[:, t * LANES:(t + 1) * LANES], c, sa, sb, HEAD_DIM // 4)
                d0, d1 = _dup_halves(r)
                k_ref[:, (2 * t) * LANES:(2 * t + 1) * LANES] = d0.astype(BF16)
                k_ref[:, (2 * t + 1) * LANES:(2 * t + 2) * LANES] = d1.astype(BF16)
        else:
            lo_half = lax.broadcasted_iota(jnp.int32, (y.shape[0], LANES), 1) < HEAD_DIM
            for t in range(2):
                for u, dup in enumerate(_dup_halves(y[:, t * LANES:(t + 1) * LANES])):
                    col = (2 * t + u) * MXU_DIM
                    v_ref[:, col:col + LANES] = jnp.where(lo_half, dup, 1.0).astype(BF16)
                    v_ref[:, col + LANES:col + 2 * LANES] = jnp.where(lo_half, 1.0, dup).astype(BF16)


def _qkv_a(x2, w, gq, gk, tabs, seq):
    n, d = x2.shape
    tm = ROW_TILE
    n_out = w.shape[1]
    c, sa, sb = tabs
    bd = (jnp.arange(MXU_DIM)[:, None] // HEAD_DIM == jnp.arange(MXU_DIM)[None, :] // HEAD_DIM)
    bd = (bd.astype(F32) / HEAD_DIM).astype(BF16)
    gq2 = jnp.tile(gq, MXU_DIM // HEAD_DIM)[None, :]
    gk2 = jnp.tile(gk, MXU_DIM // HEAD_DIM)[None, :]
    tab_spec = pl.BlockSpec((tm, LANES), lambda i: (i % (seq // tm), 0))
    kvw = A_KV_HEADS * LANES
    return pl.pallas_call(
        _qkv_a_kernel,
        grid=(n // tm,),
        in_specs=[pl.BlockSpec((tm, d), lambda i: (i, 0)),
                  _const_spec((d, n_out)), _const_spec((MXU_DIM, MXU_DIM)),
                  _const_spec((1, MXU_DIM)), _const_spec((1, MXU_DIM)),
                  tab_spec, tab_spec, tab_spec],
        out_specs=[pl.BlockSpec((tm, d), lambda i: (i, 0)),
                   pl.BlockSpec((tm, kvw), lambda i: (i, 0)),
                   pl.BlockSpec((tm, 2 * kvw), lambda i: (i, 0))],
        out_shape=[jax.ShapeDtypeStruct((n, d), BF16),
                   jax.ShapeDtypeStruct((n, kvw), BF16),
                   jax.ShapeDtypeStruct((n, 2 * kvw), BF16)],
        compiler_params=_params(1),
        name="qkv_gqa",
    )(x2, w.astype(BF16), bd, gq2, gk2, c, sa, sb)


def _attend_streams(q_list, k_ref, v_ref, v_cols, s_ref, emit):
    rows, seq = s_ref.shape[1], s_ref.shape[2]
    k = k_ref[0]

    def scores(i):
        s_ref[i % 2] = lax.dot_general(q_list[i], k, (((1,), (1,)), ((), ())), preferred_element_type=F32)

    def finish(i):
        sb = s_ref.at[i % 2]
        c0, cw = v_cols[i]
        m = jnp.broadcast_to(jnp.max(sb[...], axis=-1, keepdims=True), (rows, KEY_TILE))
        acc = None
        for kt in range(seq // KEY_TILE):
            p = jnp.exp2(sb[:, kt * KEY_TILE:(kt + 1) * KEY_TILE] - m)
            part = jnp.dot(p.astype(BF16), v_ref[0, kt * KEY_TILE:(kt + 1) * KEY_TILE, c0:c0 + cw],
                           preferred_element_type=F32)
            acc = part if acc is None else acc + part
        return acc

    scores(0)
    for i in range(len(q_list)):
        if i + 1 < len(q_list):
            scores(i + 1)
        emit(i, finish(i))


def _attn_a_kernel(q_ref, k_ref, v_ref, o_ref, s_ref, *, tq):
    lo = lax.broadcasted_iota(jnp.int32, (STREAM_ROWS, LANES), 1) < HEAD_DIM
    streams = []
    for r in range(tq // STREAM_ROWS):
        for h in range(4):
            qp = q_ref[0, r * STREAM_ROWS:(r + 1) * STREAM_ROWS, (h // 2) * LANES:(h // 2 + 1) * LANES]
            keep = lo if h % 2 == 0 else jnp.logical_not(lo)
            streams.append(jnp.where(keep, qp.astype(F32), 0.0).astype(BF16))
    even = {}

    def emit(i, raw):
        o = raw * pltpu.roll(1.0 / raw, HEAD_DIM, 1)
        r, h = divmod(i, 4)
        if h % 2 == 0:
            even[i] = o
        else:
            o_ref[0, r * STREAM_ROWS:(r + 1) * STREAM_ROWS, (h // 2) * LANES:(h // 2 + 1) * LANES] = (
                jnp.where(lo, even.pop(i - 1), o).astype(BF16))

    _attend_streams(streams, k_ref, v_ref, [((i % 2) * LANES, LANES) for i in range(len(streams))], s_ref, emit)


def _attn_a(q, kd, vd):
    b, seq, d = q.shape
    tq = STREAMS_PER_STEP // 4 * STREAM_ROWS
    return pl.pallas_call(
        functools.partial(_attn_a_kernel, tq=tq),
        grid=(b, A_KV_HEADS, seq // tq),
        in_specs=[pl.BlockSpec((1, tq, MXU_DIM), lambda bi, g, i: (bi, i, g)),
                  pl.BlockSpec((1, seq, LANES), lambda bi, g, i: (bi, 0, g)),
                  pl.BlockSpec((1, seq, MXU_DIM), lambda bi, g, i: (bi, 0, g))],
        out_specs=pl.BlockSpec((1, tq, MXU_DIM), lambda bi, g, i: (bi, i, g)),
        out_shape=jax.ShapeDtypeStruct((b, seq, d), BF16),
        scratch_shapes=[pltpu.VMEM((2, STREAM_ROWS, seq), F32)],
        compiler_params=_params(3),
        name="attn_gqa",
    )(q, kd, vd)


def _attn_c_kernel(lqk_ref, subln_ref, q_ref, k_ref, v_ref, o_ref, s_ref, *, tq, lambda_init):
    lqk = lqk_ref[...]
    lam = (jnp.exp(jnp.sum(lqk[0:1] * lqk[1:2], axis=-1, keepdims=True))
           - jnp.exp(jnp.sum(lqk[2:3] * lqk[3:4], axis=-1, keepdims=True)) + lambda_init)
    lo = lax.broadcasted_iota(jnp.int32, (STREAM_ROWS, LANES), 1) < HEAD_DIM
    streams = []
    for r in range(tq // STREAM_ROWS):
        qp = q_ref[0, r * STREAM_ROWS:(r + 1) * STREAM_ROWS, :].astype(F32)
        streams.append(jnp.where(lo, qp, 0.0).astype(BF16))
        streams.append(jnp.where(lo, 0.0, qp).astype(BF16))
    first = {}

    def emit(i, raw):
        a = raw[:, :LANES] / raw[:, LANES:]
        r, comp = divmod(i, 2)
        if comp == 0:
            first[i] = a
            return
        o = first.pop(i - 1) - lam * a
        ms = jnp.mean(o * o, axis=-1, keepdims=True)
        o = o * lax.rsqrt(ms + RMS_EPS) * subln_ref[...] * (1.0 - lambda_init)
        o_ref[0, r * STREAM_ROWS:(r + 1) * STREAM_ROWS, :] = o.astype(BF16)

    _attend_streams(streams, k_ref, v_ref, [(0, 2 * LANES)] * len(streams), s_ref, emit)


def _attn_c(q, k, v, lqk, subln, lambda_init):
    b, seq, d = q.shape
    tq = STREAMS_PER_STEP // 2 * STREAM_ROWS
    blk = lambda bi, h, i: (bi, i, h)
    kv = lambda bi, h, i: (bi, 0, h)
    return pl.pallas_call(
        functools.partial(_attn_c_kernel, tq=tq, lambda_init=lambda_init),
        grid=(b, C_HEADS, seq // tq),
        in_specs=[_const_spec((4, HEAD_DIM)), _const_spec((1, LANES)),
                  pl.BlockSpec((1, tq, LANES), blk),
                  pl.BlockSpec((1, seq, LANES), kv),
                  pl.BlockSpec((1, seq, 2 * LANES), kv)],
        out_specs=pl.BlockSpec((1, tq, LANES), blk),
        out_shape=jax.ShapeDtypeStruct((b, seq, d), BF16),
        scratch_shapes=[pltpu.VMEM((2, STREAM_ROWS, seq), F32)],
        compiler_params=_params(3),
        name="attn_diff",
    )(lqk, subln[None, :], q, k, v)


def _qkv_c_kernel(x_ref, w_ref, c_ref, sa_ref, sb_ref, q_ref, k_ref, v_ref):
    xb = x_ref[...].astype(BF16)
    c, sa, sb = c_ref[...], sa_ref[...], sb_ref[...]
    d = x_ref.shape[1]
    n_chunks = d // MXU_DIM
    for j in range(3 * n_chunks):
        y = jnp.dot(xb, w_ref[:, j * MXU_DIM:(j + 1) * MXU_DIM], preferred_element_type=F32)
        which, jj = divmod(j, n_chunks)
        if which == 2:
            for t in range(2):
                col = (2 * jj + t) * MXU_DIM
                v_ref[:, col:col + LANES] = y[:, t * LANES:(t + 1) * LANES].astype(BF16)
                v_ref[:, col + LANES:col + 2 * LANES] = jnp.ones((y.shape[0], LANES), BF16)
            continue
        for t in range(2):
            r = _rope(y[:, t * LANES:(t + 1) * LANES], c, sa, sb, PARTIAL_ROPE_DIM // 2)
            col = jj * MXU_DIM + t * LANES
            if which == 0:
                q_ref[:, col:col + LANES] = (r * ATTN_SCALE).astype(BF16)
            else:
                k_ref[:, col:col + LANES] = r.astype(BF16)


def _qkv_c(x2, w, tabs, seq):
    n, d = x2.shape
    tm = ROW_TILE
    c, sa, sb = tabs
    tab_spec = pl.BlockSpec((tm, LANES), lambda i: (i % (seq // tm), 0))
    row = pl.BlockSpec((tm, d), lambda i: (i, 0))
    return pl.pallas_call(
        _qkv_c_kernel,
        grid=(n // tm,),
        in_specs=[row, _const_spec((d, 3 * d)), tab_spec, tab_spec, tab_spec],
        out_specs=[row, row, pl.BlockSpec((tm, 2 * d), lambda i: (i, 0))],
        out_shape=[jax.ShapeDtypeStruct((n, d), BF16), jax.ShapeDtypeStruct((n, d), BF16),
                   jax.ShapeDtypeStruct((n, 2 * d), BF16)],
        compiler_params=_params(1),
        name="qkv_diff",
    )(x2, w.astype(BF16), c, sa, sb)


def _ffn_kernel(*refs, tiles_per_seq, with_mixer):
    if with_mixer:
        (m_ref, mp_ref, mn_ref, wo_ref, g1_ref, b1_ref, x_ref, xp_ref, xn_ref, wup_ref, cw_ref, cb_ref,
         wdn_ref, g_ref, b_ref, o_ref, xe_ref, h_ref, a_ref, me_ref, x1_ref) = refs
    else:
        (x_ref, xp_ref, xn_ref, wup_ref, cw_ref, cb_ref, wdn_ref, g_ref, b_ref, o_ref,
         xe_ref, h_ref, a_ref) = refs
    tm = x_ref.shape[0]
    t = pl.program_id(0) % tiles_per_seq
    if with_mixer:
        me_ref[0:HALO, :] = mp_ref[...]
        me_ref[HALO:HALO + tm, :] = m_ref[...]
        me_ref[HALO + tm:2 * HALO + tm, :] = mn_ref[...]
        sub1 = jnp.dot(me_ref[...], wo_ref[...], preferred_element_type=F32)

        def post1(xv, s):
            return _layernorm(DEEPNORM_ALPHA * xv + s, g1_ref[...], b1_ref[...])

        x1_ref[...] = post1(x_ref[...], sub1[HALO:HALO + tm])
        x1_prev = post1(xp_ref[...], sub1[0:HALO])
        x1_next = post1(xn_ref[...], sub1[HALO + tm:2 * HALO + tm])
        res_ref = x1_ref
    else:
        x1_prev, x1_next, res_ref = xp_ref[...], xn_ref[...], x_ref
    xe_ref[0:HALO, :] = jnp.where(t == 0, 0.0, x1_prev).astype(BF16)
    xe_ref[HALO:HALO + tm, :] = res_ref[...].astype(BF16)
    xe_ref[HALO + tm:2 * HALO + tm, :] = jnp.where(t == tiles_per_seq - 1, 0.0, x1_next).astype(BF16)
    d_ff = wdn_ref.shape[0]
    n_lt = FFN_CHUNK // LANES
    for ci in range(d_ff // FFN_CHUNK):
        c0 = ci * FFN_CHUNK
        for half, base in enumerate((c0, d_ff + c0)):
            h = jnp.dot(xe_ref[...], wup_ref[:, base:base + FFN_CHUNK], preferred_element_type=F32)
            for j in range(n_lt):
                h_ref[half * n_lt + j] = h[:, j * LANES:(j + 1) * LANES]

        def conv(slot, col):
            lanes = slice(col, col + LANES)
            return (cb_ref[:, lanes]
                    + h_ref[slot, HALO - 1:HALO - 1 + tm, :] * cw_ref[0:1, lanes]
                    + h_ref[slot, HALO:HALO + tm, :] * cw_ref[1:2, lanes]
                    + h_ref[slot, HALO + 1:HALO + 1 + tm, :] * cw_ref[2:3, lanes])

        for j in range(n_lt):
            gate = conv(j, c0 + j * LANES)
            val = conv(n_lt + j, d_ff + c0 + j * LANES)
            act = gate * (1.0 / (1.0 + jnp.exp(-gate))) * val
            a_ref[:, c0 + j * LANES:c0 + (j + 1) * LANES] = act.astype(BF16)
    sub = jnp.dot(a_ref[...], wdn_ref[...], preferred_element_type=F32)
    o_ref[...] = _layernorm(DEEPNORM_ALPHA * res_ref[...] + sub, g_ref[...], b_ref[...])


def _ffn(x2, w_up, conv_w, conv_b, w_down, g, b, seq, mixer=None):
    n, d = x2.shape
    tm = ROW_TILE
    d_ff = w_down.shape[0]
    assert d_ff % FFN_CHUNK == 0 and w_up.shape == (d, 2 * d_ff)
    row = pl.BlockSpec((tm, d), lambda i: (i, 0))
    hb = tm // HALO
    prev = pl.BlockSpec((HALO, d), lambda i: (jnp.maximum(i * hb - 1, 0), 0))
    nxt = pl.BlockSpec((HALO, d), lambda i: (jnp.minimum((i + 1) * hb, n // HALO - 1), 0))
    vec = _const_spec((1, d))
    in_specs = [row, prev, nxt,
                _const_spec((d, 2 * d_ff)), _const_spec((CONV_WIDTH, 2 * d_ff)), _const_spec((1, 2 * d_ff)),
                _const_spec((d_ff, d)), vec, vec]
    args = [x2, x2, x2, w_up.astype(BF16), conv_w, conv_b[None, :], w_down.astype(BF16), g[None, :], b[None, :]]
    scratch = [pltpu.VMEM((tm + 2 * HALO, d), BF16),
               pltpu.VMEM((2 * FFN_CHUNK // LANES, tm + 2 * HALO, LANES), F32),
               pltpu.VMEM((tm, d_ff), BF16)]
    if mixer is not None:
        a2, wo, g1, b1 = mixer
        in_specs = [row, prev, nxt, _const_spec((d, d)), vec, vec] + in_specs
        args = [a2, a2, a2, wo.astype(BF16), g1[None, :], b1[None, :]] + args
        scratch += [pltpu.VMEM((tm + 2 * HALO, d), BF16), pltpu.VMEM((tm, d), F32)]
    return pl.pallas_call(
        functools.partial(_ffn_kernel, tiles_per_seq=seq // tm, with_mixer=mixer is not None),
        grid=(n // tm,),
        in_specs=in_specs,
        out_specs=row,
        out_shape=jax.ShapeDtypeStruct((n, d), F32),
        scratch_shapes=scratch,
        compiler_params=_params(1),
        name="conv_ffn",
    )(*args)


FFT_RADIX = 64
FFT_MINOR = 8


def _pack_bf16_pair(re, im):
    hi = lax.bitcast_convert_type(re.astype(BF16).astype(F32), jnp.uint32)
    lo = lax.bitcast_convert_type(im.astype(BF16).astype(F32), jnp.uint32)
    return hi | (lo >> 16)


def _unpack_bf16_pair(w):
    re = lax.bitcast_convert_type(w & jnp.uint32(0xFFFF0000), F32)
    im = lax.bitcast_convert_type(w << 16, F32)
    return re.astype(BF16), im.astype(BF16)


def _dft_ab_kernel(x_ref, cs_ref, la_ref, tc_ref, ts_ref, b_ref, y_ref):
    gw = cs_ref.shape[0]
    d = gw * FNET_GROUPS
    rows = FFT_RADIX * FFT_MINOR
    xb = x_ref[0].reshape(rows, d).astype(BF16)
    for g in range(FNET_GROUPS):
        yg = jnp.dot(xb[:, g * gw:(g + 1) * gw], cs_ref[...], preferred_element_type=F32)
        y_ref[0:rows, g * gw:(g + 1) * gw] = yg[:, :gw].astype(BF16)
        y_ref[rows:2 * rows, g * gw:(g + 1) * gw] = yg[:, gw:].astype(BF16)
    a = jnp.dot(la_ref[...], y_ref[...], preferred_element_type=F32)
    a_re, a_im = a[:rows], a[rows:]
    tc = jnp.tile(tc_ref[...].reshape(rows, LANES), (1, d // LANES))
    ts = jnp.tile(ts_ref[...].reshape(rows, LANES), (1, d // LANES))
    packed = _pack_bf16_pair(a_re * tc + a_im * ts, a_im * tc - a_re * ts)
    b_ref[0] = packed.reshape(FFT_RADIX, FFT_MINOR, d)


def _dft_c_proj_ln_kernel(bw_ref, lc_ref, w_ref, bias_ref, x_ref, g_ref, b_ref, o_ref, rhs_ref):
    d = w_ref.shape[0]
    rows = FFT_RADIX * FFT_MINOR
    b_re, b_im = _unpack_bf16_pair(bw_ref[0])
    rhs_ref[0:rows, :] = b_re
    rhs_ref[rows:2 * rows, :] = b_im
    f = jnp.dot(lc_ref[...], rhs_ref[...], preferred_element_type=F32).astype(BF16)
    sub = jnp.dot(f, w_ref[...], preferred_element_type=F32) + bias_ref[...]
    y = DEEPNORM_ALPHA * x_ref[0].reshape(rows, d) + sub
    o_ref[0] = _layernorm(y, g_ref[...], b_ref[...]).reshape(FFT_RADIX, FFT_MINOR, d)


def _cos_sin_matrix(rows, cols, period):
    ang = ((rows[:, None] * cols[None, :]) % period).astype(F32) * (2.0 * math.pi / period)
    return jnp.cos(ang), jnp.sin(ang)


def _fourier_layer(x, w_o, b_o, g, bb):
    b, seq, d = x.shape
    r = FFT_RADIX
    m = FFT_MINOR
    assert seq == r * r and r % m == 0
    rows = r * m
    gw = d // FNET_GROUPS
    cidx = jnp.arange(gw, dtype=jnp.int32)
    cc, sc = _cos_sin_matrix(cidx, cidx, gw)
    cs_chan = (jnp.concatenate([cc, -sc], axis=1) * gw ** -0.5).astype(BF16)
    ridx = jnp.arange(r, dtype=jnp.int32)
    c64, s64 = (t * r ** -0.5 for t in _cos_sin_matrix(ridx, ridx, r))
    eye = jnp.eye(m, dtype=F32)
    ca, sa = jnp.kron(c64, eye), jnp.kron(s64, eye)
    l_a = jnp.concatenate([jnp.concatenate([ca, sa], axis=1),
                           jnp.concatenate([-sa, ca], axis=1)], axis=0).astype(BF16)
    expand = lambda f: jnp.einsum('kn,ab->kabn', f, eye).reshape(rows, rows)
    l_c = jnp.concatenate([expand(c64), expand(s64)], axis=1).astype(BF16)
    tc, ts = (jnp.broadcast_to(t[:, :, None], (r, r, LANES)) for t in _cos_sin_matrix(ridx, ridx, seq))
    x4 = x.reshape(b, r, r, d)
    blk = pl.BlockSpec((1, r, m, d), lambda bi, j: (bi, 0, j, 0))
    tw_blk = pl.BlockSpec((r, m, LANES), lambda bi, j: (0, j, 0))
    bw = pl.pallas_call(
        _dft_ab_kernel,
        grid=(b, r // m),
        in_specs=[blk, _const_spec((gw, 2 * gw)), _const_spec((2 * rows, 2 * rows)), tw_blk, tw_blk],
        out_specs=blk,
        out_shape=jax.ShapeDtypeStruct((b, r, r, d), jnp.uint32),
        scratch_shapes=[pltpu.VMEM((2 * rows, d), BF16)],
        compiler_params=_params(2),
        name="dft_stage_ab",
    )(x4, cs_chan, l_a, tc, ts)
    vec = _const_spec((1, d))
    out = pl.pallas_call(
        _dft_c_proj_ln_kernel,
        grid=(b, r // m),
        in_specs=[pl.BlockSpec((1, rows, d), lambda bi, j: (bi, j, 0)), _const_spec((rows, 2 * rows)),
                  _const_spec((d, d)), vec, blk, vec, vec],
        out_specs=blk,
        out_shape=jax.ShapeDtypeStruct((b, r, r, d), F32),
        scratch_shapes=[pltpu.VMEM((2 * rows, d), BF16)],
        compiler_params=_params(2),
        name="dft_stage_c_proj_ln",
    )(bw.reshape(b, seq, d), l_c, w_o.astype(BF16), b_o[None, :], x4, g[None, :], bb[None, :])
    return out.reshape(b, seq, d)


def _cos_sin(pos, dim, theta):
    inv_freq = theta ** (-jnp.arange(0, dim, 2, dtype=F32) / dim)
    ang = pos.astype(F32)[:, None] * inv_freq[None, :]
    return jnp.cos(ang), jnp.sin(ang)


def _axial_tables(seq):
    rows = seq // GRID_W
    t_row = jnp.repeat(jnp.arange(rows, dtype=jnp.int32), GRID_W)
    t_col = jnp.tile(jnp.arange(GRID_W, dtype=jnp.int32), rows)
    cr, sr = _cos_sin(t_row, HEAD_DIM // 2, AXIAL_THETA)
    cc, sc = _cos_sin(t_col, HEAD_DIM // 2, AXIAL_THETA)
    z = jnp.zeros_like(sr)
    c = jnp.concatenate([cr, cr, cc, cc], axis=1)
    sa = jnp.concatenate([-sr, z, -sc, z], axis=1)
    sb = jnp.concatenate([z, sr, z, sc], axis=1)
    return tuple(jnp.tile(t, (1, LANES // HEAD_DIM)) for t in (c, sa, sb))


def _partial_tables(seq):
    co, si = _cos_sin(jnp.arange(seq, dtype=jnp.int32), PARTIAL_ROPE_DIM, ROPE_THETA)
    rest = HEAD_DIM - PARTIAL_ROPE_DIM
    z = jnp.zeros_like(si)
    zr = jnp.zeros((seq, rest), F32)
    c = jnp.concatenate([co, co, jnp.ones((seq, rest), F32)], axis=1)
    sa = jnp.concatenate([-si, z, zr], axis=1)
    sb = jnp.concatenate([z, si, zr], axis=1)
    return tuple(jnp.tile(t, (1, LANES // HEAD_DIM)) for t in (c, sa, sb))


def _lambda_init(layer_idx):
    return 0.8 - 0.6 * math.exp(-0.3 * layer_idx)


def kernel(x, l0_a_wqkv, l0_a_qnorm, l0_a_knorm, l0_a_wo, l0_ln1_g, l0_ln1_b, l0_ffn_wup, l0_ffn_conv_w, l0_ffn_conv_b, l0_ffn_wdown, l0_ln2_g, l0_ln2_b, l1_f_wo, l1_f_bo, l1_ln1_g, l1_ln1_b, l1_ffn_wup, l1_ffn_conv_w, l1_ffn_conv_b, l1_ffn_wdown, l1_ln2_g, l1_ln2_b, l2_c_wqkv, l2_c_lq1, l2_c_lk1, l2_c_lq2, l2_c_lk2, l2_c_subln, l2_c_wo, l2_ln1_g, l2_ln1_b, l2_ffn_wup, l2_ffn_conv_w, l2_ffn_conv_b, l2_ffn_wdown, l2_ln2_g, l2_ln2_b, l3_a_wqkv, l3_a_qnorm, l3_a_knorm, l3_a_wo, l3_ln1_g, l3_ln1_b, l3_ffn_wup, l3_ffn_conv_w, l3_ffn_conv_b, l3_ffn_wdown, l3_ln2_g, l3_ln2_b):
    b, seq, d = x.shape
    n = b * seq
    assert seq % ROW_TILE == 0 and seq % (STREAMS_PER_STEP // 2 * STREAM_ROWS) == 0
    assert d == A_HEADS * HEAD_DIM
    axial = _axial_tables(seq)
    partial_t = _partial_tables(seq)

    def gqa_attention(x2, wqkv, qn, kn):
        q, kd, vd = _qkv_a(x2, wqkv, qn, kn, axial, seq)
        o = _attn_a(q.reshape(b, seq, d), kd.reshape(b, seq, -1), vd.reshape(b, seq, -1))
        return o.reshape(n, d)

    x2 = x.reshape(n, d)
    o = gqa_attention(x2, l0_a_wqkv, l0_a_qnorm, l0_a_knorm)
    x2 = _ffn(x2, l0_ffn_wup, l0_ffn_conv_w, l0_ffn_conv_b, l0_ffn_wdown, l0_ln2_g, l0_ln2_b, seq,
              mixer=(o, l0_a_wo, l0_ln1_g, l0_ln1_b))
    x2 = _fourier_layer(x2.reshape(b, seq, d), l1_f_wo, l1_f_bo, l1_ln1_g, l1_ln1_b).reshape(n, d)
    x2 = _ffn(x2, l1_ffn_wup, l1_ffn_conv_w, l1_ffn_conv_b, l1_ffn_wdown, l1_ln2_g, l1_ln2_b, seq)
    q, k, v = _qkv_c(x2, l2_c_wqkv, partial_t, seq)
    lqk = jnp.stack([l2_c_lq1, l2_c_lk1, l2_c_lq2, l2_c_lk2]).astype(F32)
    o = _attn_c(q.reshape(b, seq, d), k.reshape(b, seq, d), v.reshape(b, seq, 2 * d), lqk, l2_c_subln,
                _lambda_init(2))
    x2 = _ffn(x2, l2_ffn_wup, l2_ffn_conv_w, l2_ffn_conv_b, l2_ffn_wdown, l2_ln2_g, l2_ln2_b, seq,
              mixer=(o.reshape(n, d), l2_c_wo, l2_ln1_g, l2_ln1_b))
    o = gqa_attention(x2, l3_a_wqkv, l3_a_qnorm, l3_a_knorm)
    x2 = _ffn(x2, l3_ffn_wup, l3_ffn_conv_w, l3_ffn_conv_b, l3_ffn_wdown, l3_ln2_g, l3_ln2_b, seq,
              mixer=(o, l3_a_wo, l3_ln1_g, l3_ln1_b))
    return x2.reshape(b, seq, d)
```
